```python
import math
import numpy as np
import jax
import jax.numpy as jnp
from jax import lax

D_MODEL = 2048
BATCH = 16
SEQ = 2048
DEPTH = 2

GRID_W = 64
CTX_LEN = 256
D_MIX = D_MODEL
SSD_HEADS = 12
SSD_HEADDIM = 64
SSD_W = SSD_HEADS * SSD_HEADDIM
SSD_GROUPS = 2
SSD_HPG = SSD_HEADS // SSD_GROUPS
SSD_STATE = 128
SSD_CONV = 4
SSD_CHUNK = 128
SSD_CONV_CH = SSD_W + 2 * SSD_GROUPS * SSD_STATE
POOL_WINDOWS = (2, 4, 8, 16)
POOL_GROUPS = 4
POOL_GC = 128
POOL_W = POOL_GROUPS * POOL_GC
MLA_HEADS = 6
MLA_NOPE = 128
MLA_ROPE = 64
MLA_QK = MLA_NOPE + MLA_ROPE
MLA_V = 128
MLA_W = MLA_HEADS * MLA_V
Q_LORA = 512
KV_LORA = 512
ROPE_BASE = 10000.0
Q_BLOCK = 128
FFN_DIM = 5632
N_EXPERTS = 8
TOP_K = 2
N_DENSE = (DEPTH + 1) // 2
N_MOE = DEPTH // 2
NORM_EPS = 1e-6
IN_SIZES = (SSD_W, SSD_CONV_CH, 2 * SSD_HEADS, POOL_W, Q_LORA, KV_LORA, MLA_ROPE)
D_IN = sum(IN_SIZES)

kernel_name = 'hybrid_ssd_pool_mla_prefix_dit_block'


def rms_norm(x, w):
    xf = x.astype(jnp.float32)
    y = xf * lax.rsqrt(jnp.mean(xf * xf, axis=-1, keepdims=True) + NORM_EPS)
    return y.astype(x.dtype) * w


def modulate(h, shift, scale):
    return h * (1.0 + scale) + shift


def split_cols(p, sizes):
    idx = np.cumsum(sizes)[:-1].tolist()
    return jnp.split(p, idx, axis=-1)


def rev(t):
    return jnp.flip(t, axis=1)


def dwconv(u, w, b):
    y = lax.conv_general_dilated(u, w[:, None, :], window_strides=(1,),
                                 padding=[(SSD_CONV // 2, SSD_CONV - 1 - SSD_CONV // 2)],
                                 dimension_numbers=('NWC', 'WIO', 'NWC'),
                                 feature_group_count=u.shape[-1])
    return y + b


def segsum(a):
    t = a.shape[-1]
    ae = jnp.broadcast_to(a[..., :, None], a.shape + (t,))
    strict = jnp.tril(jnp.ones((t, t), dtype=bool), -1)
    s = jnp.cumsum(jnp.where(strict, ae, 0.0), axis=-2)
    return jnp.where(jnp.tril(jnp.ones((t, t), dtype=bool)), s, -jnp.inf)


def ssd_scan(xh, dt, a, bm, cm, init_state):
    bsz, n = xh.shape[:2]
    nc = n // SSD_CHUNK
    x = xh.reshape(bsz, nc, SSD_CHUNK, SSD_GROUPS, SSD_HPG, SSD_HEADDIM)
    dtc = dt.reshape(bsz, nc, SSD_CHUNK, SSD_GROUPS, SSD_HPG)
    bc = bm.reshape(bsz, nc, SSD_CHUNK, SSD_GROUPS, SSD_STATE)
    cc = cm.reshape(bsz, nc, SSD_CHUNK, SSD_GROUPS, SSD_STATE)
    xdt = x * dtc[..., None].astype(x.dtype)
    adt = jnp.moveaxis(dtc * a, 2, -1)
    a_cs = jnp.cumsum(adt, axis=-1)
    decay_in = jnp.exp(segsum(adt)).astype(x.dtype)
    cb = jnp.einsum('bclgn,bcsgn->bcgls', cc, bc)
    y_diag = jnp.einsum('bcgrls,bcsgrp->bclgrp', cb[:, :, :, None] * decay_in, xdt)
    decay_to_end = jnp.exp(a_cs[..., -1:] - a_cs).astype(x.dtype)
    states = jnp.einsum('bcsgn,bcgrs,bcsgrp->bcgrpn', bc, decay_to_end, xdt)
    states = jnp.concatenate([init_state[:, None].astype(states.dtype), states], axis=1)
    a_chunk = jnp.pad(jnp.moveaxis(a_cs[..., -1], 1, -1), [(0, 0), (0, 0), (0, 0), (1, 0)])
    decay_chunk = jnp.exp(segsum(a_chunk)).astype(x.dtype)
    states = jnp.einsum('bgrzc,bcgrpn->bzgrpn', decay_chunk, states)
    decay_from_start = jnp.exp(a_cs).astype(x.dtype)
    y_off = jnp.einsum('bclgn,bcgrpn,bcgrl->bclgrp', cc, states[:, :-1], decay_from_start)
    y = (y_diag + y_off).reshape(bsz, n, SSD_GROUPS, SSD_HPG, SSD_HEADDIM)
    return y, states[:, -1]


def ssd_inputs(xbc_raw, dt_raw, conv_w, conv_b, dt_bias):
    bsz, n = xbc_raw.shape[:2]
    xbc = jax.nn.silu(dwconv(xbc_raw, conv_w, conv_b))
    xs, bm, cm = split_cols(xbc, (SSD_W, SSD_GROUPS * SSD_STATE, SSD_GROUPS * SSD_STATE))
    xs = xs.reshape(bsz, n, SSD_GROUPS, SSD_HPG, SSD_HEADDIM)
    bm = bm.reshape(bsz, n, SSD_GROUPS, SSD_STATE)
    cm = cm.reshape(bsz, n, SSD_GROUPS, SSD_STATE)
    dt = jax.nn.softplus(dt_raw.astype(jnp.float32).reshape(bsz, n, 2, SSD_HEADS)
                         + dt_bias.astype(jnp.float32))
    return xs, bm, cm, dt.reshape(bsz, n, 2, SSD_GROUPS, SSD_HPG)


def ssd_output(y, z, norm_w):
    bsz, n = y.shape[:2]
    return rms_norm(y.reshape(bsz, n, SSD_W) * jax.nn.silu(z), norm_w)


def multiscale_pool(u, pool_w, pool_scale):
    bsz, n, _ = u.shape
    uf = u.astype(jnp.float32)
    cs = jnp.concatenate([jnp.zeros((bsz, 1, POOL_W), jnp.float32), jnp.cumsum(uf, axis=1)], axis=1)
    t = jnp.arange(n)
    means = []
    for g, w in enumerate(POOL_WINDOWS):
        lo = jnp.clip(t - w // 2, 0, n)
        hi = jnp.clip(t - w // 2 + w, 0, n)
        cg = cs[:, :, g * POOL_GC:(g + 1) * POOL_GC]
        win_sum = jnp.take(cg, hi, axis=1) - jnp.take(cg, lo, axis=1)
        means.append(win_sum / (hi - lo).astype(jnp.float32)[None, :, None])
    pooled = (jnp.concatenate(means, axis=-1) - uf).astype(u.dtype)
    pooled = pooled.reshape(bsz, n, POOL_GROUPS, POOL_GC)
    y = jnp.einsum('blgc,gcd->blgd', pooled, pool_w).reshape(bsz, n, POOL_W)
    return y * pool_scale


def axial_rope(u, pos_row, pos_col):
    quarter = MLA_ROPE // 4
    inv = ROPE_BASE ** (-jnp.arange(quarter, dtype=jnp.float32) / quarter)
    ang = jnp.stack([pos_row, pos_col], axis=-1)[:, :, None] * inv
    cos = jnp.cos(ang)[None, :, None].astype(u.dtype)
    sin = jnp.sin(ang)[None, :, None].astype(u.dtype)
    ur = u.reshape(u.shape[:-1] + (2, 2, quarter))
    u1, u2 = ur[..., 0, :], ur[..., 1, :]
    out = jnp.stack([u1 * cos - u2 * sin, u2 * cos + u1 * sin], axis=-2)
    return out.reshape(u.shape)


def rope_part(t, pos):
    if pos is None:
        return t
    return jnp.concatenate([t[..., :MLA_NOPE], axial_rope(t[..., MLA_NOPE:], pos[0], pos[1])], axis=-1)


def mla_q(cq, q_norm_w, w_uq, qh_norm_w, pos):
    bsz, n = cq.shape[:2]
    q = (rms_norm(cq, q_norm_w) @ w_uq).reshape(bsz, n, MLA_HEADS, MLA_QK)
    return rope_part(rms_norm(q, qh_norm_w), pos)


def mla_kv(ckv, krope, kv_norm_w, w_ukv, kh_norm_w, pos):
    bsz, n = ckv.shape[:2]
    kv = (rms_norm(ckv, kv_norm_w) @ w_ukv).reshape(bsz, n, MLA_HEADS, MLA_NOPE + MLA_V)
    k_nope, v = kv[..., :MLA_NOPE], kv[..., MLA_NOPE:]
    k_r = jnp.broadcast_to(krope[:, :, None, :], (bsz, n, MLA_HEADS, MLA_ROPE))
    k = rms_norm(jnp.concatenate([k_nope, k_r], axis=-1), kh_norm_w)
    return rope_part(k, pos), v


def block_attention(q, k, v):
    bsz, lq, h, e = q.shape
    qb = jnp.moveaxis(q.reshape(bsz, lq // Q_BLOCK, Q_BLOCK, h, e), 1, 0)
    scale = e ** -0.5

    def one_block(qi):
        s = jnp.einsum('bqhe,bkhe->bhqk', qi, k).astype(jnp.float32) * scale
        p = jax.nn.softmax(s, axis=-1).astype(v.dtype)
        return jnp.einsum('bhqk,bkhd->bqhd', p, v)

    o = lax.map(one_block, qb)
    return jnp.moveaxis(o, 0, 1).reshape(bsz, lq, h * v.shape[-1])


def token_mixers(u_l, u_c, pos, with_ctx_out, w_in, conv_w, conv_b, dt_bias, a_log, d_skip,
                 ssd_norm_w, pool_w, pool_scale, q_norm_w, w_uq, kv_norm_w, w_ukv,
                 qh_norm_w, kh_norm_w, w_out):
    z_l, xbc_l, dtr_l, pool_l, cq_l, ckv_l, kr_l = split_cols(u_l @ w_in, IN_SIZES)
    z_c, xbc_c, dtr_c, pool_c, cq_c, ckv_c, kr_c = split_cols(u_c @ w_in, IN_SIZES)
    xs_l, b_l, c_l, dt_l = ssd_inputs(xbc_l, dtr_l, conv_w, conv_b, dt_bias)
    xs_c, b_c, c_c, dt_c = ssd_inputs(xbc_c, dtr_c, conv_w, conv_b, dt_bias)
    a = -jnp.exp(a_log.astype(jnp.float32)).reshape(2, SSD_GROUPS, SSD_HPG)
    zero = jnp.zeros((u_l.shape[0], SSD_GROUPS, SSD_HPG, SSD_HEADDIM, SSD_STATE), xs_l.dtype)
    yf_c, sf_c = ssd_scan(xs_c, dt_c[:, :, 0], a[0], b_c, c_c, zero)
    yf_l, _ = ssd_scan(xs_l, dt_l[:, :, 0], a[0], b_l, c_l, sf_c)
    yb_c, sb_c = ssd_scan(rev(xs_c), rev(dt_c[:, :, 1]), a[1], rev(b_c), rev(c_c), zero)
    yb_l, _ = ssd_scan(rev(xs_l), rev(dt_l[:, :, 1]), a[1], rev(b_l), rev(c_l), sb_c)
    d = d_skip.reshape(SSD_GROUPS, SSD_HPG, 1)
    ssd_l = ssd_output(yf_l + rev(yb_l) + xs_l * d, z_l, ssd_norm_w)
    pool_out_l = multiscale_pool(pool_l, pool_w, pool_scale)
    q_l = mla_q(cq_l, q_norm_w, w_uq, qh_norm_w, pos)
    k_l, v_l = mla_kv(ckv_l, kr_l, kv_norm_w, w_ukv, kh_norm_w, pos)
    k_c, v_c = mla_kv(ckv_c, kr_c, kv_norm_w, w_ukv, kh_norm_w, None)
    att_l = block_attention(q_l, jnp.concatenate([k_c, k_l], axis=1), jnp.concatenate([v_c, v_l], axis=1))
    m_l = jnp.concatenate([ssd_l, pool_out_l, att_l], axis=-1) @ w_out
    if not with_ctx_out:
        return m_l, None
    ssd_c = ssd_output(yf_c + rev(yb_c) + xs_c * d, z_c, ssd_norm_w)
    pool_out_c = multiscale_pool(pool_c, pool_w, pool_scale)
    att_c = block_attention(mla_q(cq_c, q_norm_w, w_uq, qh_norm_w, None), k_c, v_c)
    m_c = jnp.concatenate([ssd_c, pool_out_c, att_c], axis=-1) @ w_out
    return m_l, m_c


def swiglu(v, w_gate, w_up, w_down):
    return (jax.nn.silu(v @ w_gate) * (v @ w_up)) @ w_down


def moe_swiglu(v, w_router, w_gate, w_up, w_down):
    logits = (v @ w_router).astype(jnp.float32)
    top_val, top_idx = lax.top_k(logits, TOP_K)
    gates = jax.nn.softmax(top_val, axis=-1)
    weights = jnp.sum(jax.nn.one_hot(top_idx, N_EXPERTS, dtype=jnp.float32) * gates[..., None], axis=-2)
    out = jnp.zeros_like(v)
    for e in range(N_EXPERTS):
        out = out + weights[..., e:e + 1].astype(v.dtype) * swiglu(v, w_gate[e], w_up[e], w_down[e])
    return out


def channel_mixer(v, layer, ffn_w_gate, ffn_w_up, ffn_w_down, moe_w_router, moe_w_gate, moe_w_up, moe_w_down):
    j = layer // 2
    if layer % 2 == 0:
        return swiglu(v, ffn_w_gate[j], ffn_w_up[j], ffn_w_down[j])
    return moe_swiglu(v, moe_w_router[j], moe_w_gate[j], moe_w_up[j], moe_w_down[j])


def setup_inputs(seed: int = 0) -> dict:
    key = jax.random.key(seed)
    ks = iter(jax.random.split(key, 40))

    def nrm(shape, scale):
        return scale * jax.random.normal(next(ks), shape, jnp.float32)

    L = DEPTH
    dt0 = jnp.exp(jax.random.uniform(next(ks), (L, 2, SSD_HEADS), jnp.float32,
                                     minval=math.log(1e-3), maxval=math.log(1e-1)))
    dt_bias = dt0 + jnp.log(-jnp.expm1(-dt0))
    a_log = jnp.log(jax.random.uniform(next(ks), (L, 2, SSD_HEADS), jnp.float32, minval=1.0, maxval=16.0))
    return {
        'x': nrm((BATCH, SEQ, D_MODEL), 1.0),
        'c': nrm((BATCH, D_MODEL), 1.0),
        'ctx': nrm((BATCH, CTX_LEN, D_MODEL), 1.0),
        'c_ctx': nrm((D_MODEL,), 1.0),
        'ada_w': nrm((L, D_MODEL, 6 * D_MODEL), 0.5 * D_MODEL ** -0.5),
        'ada_b': nrm((L, 6 * D_MODEL), 0.01),
        'norm_mix_w': 1.0 + nrm((L, D_MODEL), 0.02),
        'norm_ffn_w': 1.0 + nrm((L, D_MODEL), 0.02),
        'w_in': nrm((L, D_MODEL, D_IN), D_MODEL ** -0.5),
        'ssd_conv_w': nrm((L, SSD_CONV, SSD_CONV_CH), SSD_CONV ** -0.5),
        'ssd_conv_b': nrm((L, SSD_CONV_CH), 0.01),
        'ssd_dt_bias': dt_bias,
        'ssd_a_log': a_log,
        'ssd_d': 1.0 + nrm((L, SSD_HEADS), 0.1),
        'ssd_norm_w': 1.0 + nrm((L, SSD_W), 0.02),
        'pool_w': nrm((L, POOL_GROUPS, POOL_GC, POOL_GC), POOL_GC ** -0.5),
        'pool_scale': 1.0 + nrm((L, POOL_W), 0.1),
        'mla_q_norm_w': 1.0 + nrm((L, Q_LORA), 0.02),
        'mla_w_uq': nrm((L, Q_LORA, MLA_HEADS * MLA_QK), Q_LORA ** -0.5),
        'mla_kv_norm_w': 1.0 + nrm((L, KV_LORA), 0.02),
        'mla_w_ukv': nrm((L, KV_LORA, MLA_HEADS * (MLA_NOPE + MLA_V)), KV_LORA ** -0.5),
        'mla_q_head_norm_w': 1.0 + nrm((L, MLA_QK), 0.02),
        'mla_k_head_norm_w': 1.0 + nrm((L, MLA_QK), 0.02),
        'w_out': nrm((L, D_MIX, D_MODEL), D_MIX ** -0.5),
        'ffn_w_gate': nrm((N_DENSE, D_MODEL, FFN_DIM), D_MODEL ** -0.5),
        'ffn_w_up': nrm((N_DENSE, D_MODEL, FFN_DIM), D_MODEL ** -0.5),
        'ffn_w_down': nrm((N_DENSE, FFN_DIM, D_MODEL), FFN_DIM ** -0.5),
        'moe_w_router': nrm((N_MOE, D_MODEL, N_EXPERTS), D_MODEL ** -0.5),
        'moe_w_gate': nrm((N_MOE, N_EXPERTS, D_MODEL, FFN_DIM), D_MODEL ** -0.5),
        'moe_w_up': nrm((N_MOE, N_EXPERTS, D_MODEL, FFN_DIM), D_MODEL ** -0.5),
        'moe_w_down': nrm((N_MOE, N_EXPERTS, FFN_DIM, D_MODEL), FFN_DIM ** -0.5),
    }


def reference(x, c, ctx, c_ctx, ada_w, ada_b, norm_mix_w, norm_ffn_w, w_in, ssd_conv_w, ssd_conv_b,
              ssd_dt_bias, ssd_a_log, ssd_d, ssd_norm_w, pool_w, pool_scale, mla_q_norm_w, mla_w_uq,
              mla_kv_norm_w, mla_w_ukv, mla_q_head_norm_w, mla_k_head_norm_w, w_out, ffn_w_gate,
              ffn_w_up, ffn_w_down, moe_w_router, moe_w_gate, moe_w_up, moe_w_down):
    seq = x.shape[1]
    rows = seq // GRID_W
    pos_row = jnp.repeat(jnp.arange(rows, dtype=jnp.float32), GRID_W)
    pos_col = jnp.tile(jnp.arange(GRID_W, dtype=jnp.float32), rows)
    pos = (pos_row, pos_col)
    s_lat = jax.nn.silu(c)[:, None, :]
    s_ctx = jax.nn.silu(c_ctx)[None, None, :]
    h_l, h_c = x, ctx
    for i in range(DEPTH):
        last = i == DEPTH - 1
        mod_l = jnp.split(s_lat @ ada_w[i] + ada_b[i], 6, axis=-1)
        mod_c = jnp.split(s_ctx @ ada_w[i] + ada_b[i], 6, axis=-1)
        u_l = modulate(rms_norm(h_l, norm_mix_w[i]), mod_l[0], mod_l[1])
        u_c = modulate(rms_norm(h_c, norm_mix_w[i]), mod_c[0], mod_c[1])
        m_l, m_c = token_mixers(u_l, u_c, pos, not last, w_in[i], ssd_conv_w[i], ssd_conv_b[i],
                                ssd_dt_bias[i], ssd_a_log[i], ssd_d[i], ssd_norm_w[i], pool_w[i],
                                pool_scale[i], mla_q_norm_w[i], mla_w_uq[i], mla_kv_norm_w[i],
                                mla_w_ukv[i], mla_q_head_norm_w[i], mla_k_head_norm_w[i], w_out[i])
        h_l = h_l + mod_l[2] * m_l
        v_l = modulate(rms_norm(h_l, norm_ffn_w[i]), mod_l[3], mod_l[4])
        h_l = h_l + mod_l[5] * channel_mixer(v_l, i, ffn_w_gate, ffn_w_up, ffn_w_down,
                                             moe_w_router, moe_w_gate, moe_w_up, moe_w_down)
        if not last:
            h_c = h_c + mod_c[2] * m_c
            v_c = modulate(rms_norm(h_c, norm_ffn_w[i]), mod_c[3], mod_c[4])
            h_c = h_c + mod_c[5] * channel_mixer(v_c, i, ffn_w_gate, ffn_w_up, ffn_w_down,
                                                 moe_w_router, moe_w_gate, moe_w_up, moe_w_down)
    return h_l
```

```python
import functools

import numpy as np
import jax
import jax.numpy as jnp
from jax import lax
from jax.experimental import pallas as pl
from jax.experimental.pallas import tpu as pltpu

F32 = jnp.float32
BF16 = jnp.bfloat16

D_MODEL = 2048
GRID_W = 64
SSD_HEADS = 12
SSD_HEADDIM = 64
SSD_W = SSD_HEADS * SSD_HEADDIM
SSD_GROUPS = 2
SSD_HPG = SSD_HEADS // SSD_GROUPS
SSD_STATE = 128
SSD_CONV = 4
SSD_CHUNK = 128
SSD_BC = SSD_GROUPS * SSD_STATE
SSD_CONV_CH = SSD_W + 2 * SSD_BC
SSD_GW = SSD_HPG * SSD_HEADDIM
POOL_WINDOWS = (2, 4, 8, 16)
POOL_GROUPS = 4
POOL_GC = 128
POOL_W = POOL_GROUPS * POOL_GC
POOL_HALO = 8
MLA_HEADS = 6
MLA_NOPE = 128
MLA_ROPE = 64
MLA_QK = MLA_NOPE + MLA_ROPE
MLA_QKP = 256
MLA_V = 128
MLA_W = MLA_HEADS * MLA_V
Q_LORA = 512
KV_LORA = 512
ROPE_BASE = 10000.0
FFN_DIM = 5632
N_EXPERTS = 8
TOP_K = 2
NORM_EPS = 1e-6
IN_SIZES = (SSD_W, SSD_CONV_CH, 2 * SSD_HEADS, POOL_W, Q_LORA, KV_LORA, MLA_ROPE)
LANES = 128
SUBLANES = 8
IN_Z = 0
IN_XBC = IN_Z + SSD_W
IN_POOL = IN_XBC + SSD_CONV_CH
IN_CQ = IN_POOL + POOL_W
IN_CKV = IN_CQ + Q_LORA
IN_KR = IN_CKV + KV_LORA
IN_DT0 = IN_KR + LANES
IN_DT1 = IN_DT0 + LANES
IN_COLS = IN_DT1 + LANES
MIB = 1024 * 1024


def _cparams(sem, vmem_mib):
    return pltpu.CompilerParams(dimension_semantics=sem, vmem_limit_bytes=vmem_mib * MIB)


def _sigmoid(x):
    return 1.0 / (1.0 + jnp.exp(-x))


def _silu(x):
    return x * _sigmoid(x)


def _softplus(x):
    return jnp.maximum(x, 0.0) + jnp.log1p(jnp.exp(-jnp.abs(x)))


def _rms(x, w):
    return x * lax.rsqrt(jnp.mean(x * x, axis=-1, keepdims=True) + NORM_EPS) * w


def _bdot(a, b):
    return jnp.dot(a, b, preferred_element_type=F32)


def _split_bf16(x, passes):
    parts = []
    r = x
    for _ in range(passes):
        p = r.astype(BF16)
        parts.append(p)
        r = r - p.astype(F32)
    return parts


def _ldot(m_bf16, x, passes):
    out = None
    for p in _split_bf16(x, passes):
        t = _bdot(m_bf16, p)
        out = t if out is None else out + t
    return out


def _rdot(x, m_bf16, passes):
    out = None
    for p in _split_bf16(x, passes):
        t = _bdot(p, m_bf16)
        out = t if out is None else out + t
    return out


def _ada_kernel(c_ref, w_ref, b_ref, o_ref):
    s = _silu(c_ref[...]).astype(BF16)
    o_ref[...] = _bdot(s, w_ref[...].astype(BF16)) + b_ref[...]


def _ada(cvec, w, b):
    rows, d = cvec.shape
    n = w.shape[1]
    tn = 1024
    return pl.pallas_call(
        _ada_kernel,
        grid=(n // tn,),
        in_specs=[pl.BlockSpec((rows, d), lambda j: (0, 0)),
                  pl.BlockSpec((d, tn), lambda j: (0, j)),
                  pl.BlockSpec((1, tn), lambda j: (0, j))],
        out_specs=pl.BlockSpec((rows, tn), lambda j: (0, j)),
        out_shape=jax.ShapeDtypeStruct((rows, n), F32),
        compiler_params=_cparams(("arbitrary",), 40),
        name="ada",
    )(cvec, w, b)


def _in_proj_kernel(h_ref, mod_ref, nw_ref, w_ref, z_ref, xbc_ref, pool_ref, cq_ref, ckv_ref,
                    kr_ref, dt_ref):
    mod = mod_ref[0]
    u = _rms(h_ref[...], nw_ref[...]) * (1.0 + mod[1:2, :]) + mod[0:1, :]
    ub = u.astype(BF16)
    z_ref[...] = _bdot(ub, w_ref[:, IN_Z:IN_XBC])
    xbc_ref[...] = _bdot(ub, w_ref[:, IN_XBC:IN_POOL])
    pool_ref[...] = _bdot(ub, w_ref[:, IN_POOL:IN_CQ])
    cq_ref[...] = _bdot(ub, w_ref[:, IN_CQ:IN_CKV])
    ckv_ref[...] = _bdot(ub, w_ref[:, IN_CKV:IN_KR])
    kr_ref[...] = _bdot(ub, w_ref[:, IN_KR:IN_DT0])
    dt_ref[0] = _bdot(ub, w_ref[:, IN_DT0:IN_DT1])
    dt_ref[1] = _bdot(ub, w_ref[:, IN_DT1:IN_COLS])


def _in_proj(h, mod, norm_w, w, rows_per_mod):
    t, d = h.shape
    tm = 256
    tiles_per_mod = rows_per_mod // tm
    widths = (SSD_W, SSD_CONV_CH, POOL_W, Q_LORA, KV_LORA, LANES)
    row = lambda i: (i, 0)
    return pl.pallas_call(
        _in_proj_kernel,
        grid=(t // tm,),
        in_specs=[pl.BlockSpec((tm, d), row),
                  pl.BlockSpec((1, 6, d), lambda i: (i // tiles_per_mod, 0, 0)),
                  pl.BlockSpec((1, d), lambda i: (0, 0)),
                  pl.BlockSpec((d, IN_COLS), lambda i: (0, 0))],
        out_specs=[pl.BlockSpec((tm, wd), row) for wd in widths]
        + [pl.BlockSpec((2, tm, LANES), lambda i: (0, i, 0))],
        out_shape=[jax.ShapeDtypeStruct((t, wd), F32) for wd in widths]
        + [jax.ShapeDtypeStruct((2, t, LANES), F32)],
        compiler_params=_cparams(("parallel",), 56),
        name="in_proj",
    )(h, mod, norm_w, w)


def _ssd_kernel(xm_ref, xp_ref, xn_ref, z_ref, dt_ref, init_ref, cw_ref, cb_ref, dtb_ref, alog_ref,
                dsk_ref, nw_ref, e_ref, y_ref, fin_ref, yf_scr, st_scr, xe_scr, *, nc):
    q = SSD_CHUNK
    t = pl.program_id(1)
    d = t // nc
    cpos = t % nc
    c = jnp.where(d == 0, cpos, nc - 1 - cpos)

    @pl.when(cpos == 0)
    def _():
        st_scr[...] = init_ref[0, 0]

    xe_scr[0:SUBLANES, :] = jnp.where(c == 0, 0.0, xp_ref[...])
    xe_scr[SUBLANES:SUBLANES + q, :] = xm_ref[...]
    xe_scr[SUBLANES + q:, :] = jnp.where(c == nc - 1, 0.0, xn_ref[...])
    cw = cw_ref[...]
    off = SUBLANES - SSD_CONV // 2
    acc = cb_ref[...]
    for k in range(SSD_CONV):
        acc = acc + cw[k:k + 1, :] * xe_scr[off + k:off + k + q, :]
    xbc = _silu(acc)
    xs = xbc[:, :SSD_W]

    lane = lax.broadcasted_iota(jnp.int32, (q, LANES), 1)
    row = lax.broadcasted_iota(jnp.int32, (q, LANES), 0)
    dt = jnp.where(lane < SSD_HEADS, _softplus(dt_ref[0] + dtb_ref[0]), 0.0)
    adt = dt * (-jnp.exp(alog_ref[0]))
    causal = (row - lane) * (1 - 2 * d) >= 0
    tri = jnp.where(causal, 1.0, 0.0).astype(BF16)
    cs = _ldot(tri, adt, 3)
    tot = jnp.sum(adt, axis=0, keepdims=True)
    e = e_ref[...]
    dtx = _rdot(dt, e, 2)
    ecsx = _rdot(jnp.exp(cs), e, 2)
    wx = _rdot(dt * jnp.exp(tot - cs), e, 2)
    etot = _rdot(jnp.broadcast_to(jnp.exp(tot), (SUBLANES, LANES)), e, 2)[0:1, :]
    xdt = xs * dtx
    xw = (xs * wx).astype(BF16)
    cs_t = cs.T
    low_half = lane < SSD_HEADDIM

    ys = []
    for g in range(SSD_GROUPS):
        bg = xbc[:, SSD_W + g * SSD_STATE:SSD_W + (g + 1) * SSD_STATE]
        cg = xbc[:, SSD_W + SSD_BC + g * SSD_STATE:SSD_W + SSD_BC + (g + 1) * SSD_STATE]
        cgb = cg.astype(BF16)
        cb = lax.dot_general(cgb, bg.astype(BF16), (((1,), (1,)), ((), ())),
                             preferred_element_type=F32)
        gs = slice(g * SSD_GW, (g + 1) * SSD_GW)
        st = st_scr[g]
        y_off = _bdot(cgb, st.astype(BF16)) * ecsx[:, gs]
        st_scr[g] = st * etot[:, gs] + _bdot(bg.T.astype(BF16), xw[:, gs])
        for pr in range(SSD_HPG // 2):
            xpair = xdt[:, g * SSD_GW + pr * LANES:g * SSD_GW + (pr + 1) * LANES]
            y_pair = y_off[:, pr * LANES:(pr + 1) * LANES]
            for k in range(2):
                h = g * SSD_HPG + pr * 2 + k
                dcs = cs[:, h:h + 1] - cs_t[h:h + 1, :]
                m = (cb * jnp.exp(jnp.where(causal, dcs, -1e30))).astype(BF16)
                xh = jnp.where(low_half if k == 0 else jnp.logical_not(low_half), xpair, 0.0)
                y_pair = y_pair + _bdot(m, xh.astype(BF16))
            ys.append(y_pair)
    y = jnp.concatenate(ys, axis=1)

    r0 = pl.multiple_of(c * q, q)

    @pl.when(d == 0)
    def _():
        yf_scr[pl.ds(r0, q), :] = y

    @pl.when(d == 1)
    def _():
        g_ = (yf_scr[pl.ds(r0, q), :] + y + xs * dsk_ref[...]) * _silu(z_ref[...])
        y_ref[...] = _rms(g_, nw_ref[...]).astype(BF16)

    @pl.when(cpos == nc - 1)
    def _():
        fin_ref[0, 0] = st_scr[...]


def _ssd(xbc, z, dt, init, conv_w, conv_b, dt_bias, a_log, d_skip, norm_w, expand, bsz):
    t = xbc.shape[0]
    q = SSD_CHUNK
    nc = t // bsz // q
    tiles8 = q // SUBLANES

    def chunk(tt):
        cp = tt % nc
        return jnp.where(tt < nc, cp, nc - 1 - cp)

    def out_chunk(tt):
        return jnp.where(tt < nc, nc - 1, 2 * nc - 1 - tt)

    const2 = lambda b, tt: (0, 0)
    state_spec = pl.BlockSpec((1, 1, SSD_GROUPS, SSD_STATE, SSD_GW), lambda b, tt: (b, tt // nc, 0, 0, 0))
    state_shape = jax.ShapeDtypeStruct((bsz, 2, SSD_GROUPS, SSD_STATE, SSD_GW), F32)
    return pl.pallas_call(
        functools.partial(_ssd_kernel, nc=nc),
        grid=(bsz, 2 * nc),
        in_specs=[
            pl.BlockSpec((q, SSD_CONV_CH), lambda b, tt: (b * nc + chunk(tt), 0)),
            pl.BlockSpec((SUBLANES, SSD_CONV_CH),
                         lambda b, tt: (jnp.maximum((b * nc + chunk(tt)) * tiles8 - 1, 0), 0)),
            pl.BlockSpec((SUBLANES, SSD_CONV_CH),
                         lambda b, tt: (jnp.minimum((b * nc + chunk(tt) + 1) * tiles8, t // SUBLANES - 1), 0)),
            pl.BlockSpec((q, SSD_W), lambda b, tt: (b * nc + out_chunk(tt), 0)),
            pl.BlockSpec((1, q, LANES), lambda b, tt: (tt // nc, b * nc + chunk(tt), 0)),
            state_spec,
            pl.BlockSpec((SSD_CONV, SSD_CONV_CH), const2),
            pl.BlockSpec((1, SSD_CONV_CH), const2),
            pl.BlockSpec((1, 1, LANES), lambda b, tt: (tt // nc, 0, 0)),
            pl.BlockSpec((1, 1, LANES), lambda b, tt: (tt // nc, 0, 0)),
            pl.BlockSpec((1, SSD_W), const2),
            pl.BlockSpec((1, SSD_W), const2),
            pl.BlockSpec((LANES, SSD_W), const2),
        ],
        out_specs=[pl.BlockSpec((q, SSD_W), lambda b, tt: (b * nc + out_chunk(tt), 0)), state_spec],
        out_shape=[jax.ShapeDtypeStruct((t, SSD_W), BF16), state_shape],
        scratch_shapes=[pltpu.VMEM((nc * q, SSD_W), F32),
                        pltpu.VMEM((SSD_GROUPS, SSD_STATE, SSD_GW), F32),
                        pltpu.VMEM((q + 2 * SUBLANES, SSD_CONV_CH), F32)],
        compiler_params=_cparams(("parallel", "arbitrary"), 40),
        name="ssd",
    )(xbc, xbc, xbc, z, dt, init, conv_w, conv_b, dt_bias, a_log, d_skip, norm_w, expand)


def _pool_kernel(um_ref, up_ref, un_ref, pw_ref, ps_ref, o_ref, pe_scr, *, seq):
    tp = um_ref.shape[0]
    j = pl.program_id(1)
    pe_scr[0:POOL_HALO, :] = jnp.where(j == 0, 0.0, up_ref[...])
    pe_scr[POOL_HALO:POOL_HALO + tp, :] = um_ref[...]
    pe_scr[POOL_HALO + tp:, :] = jnp.where(j == pl.num_programs(1) - 1, 0.0, un_ref[...])
    pos = j * tp + lax.broadcasted_iota(jnp.int32, (tp, 1), 0)
    for g, w in enumerate(POOL_WINDOWS):
        cols = slice(g * POOL_GC, (g + 1) * POOL_GC)
        start = POOL_HALO - w // 2
        win = pe_scr[start:start + tp, cols]
        for k in range(1, w):
            win = win + pe_scr[start + k:start + k + tp, cols]
        lo = jnp.maximum(pos - w // 2, 0)
        hi = jnp.minimum(pos - w // 2 + w, seq)
        pooled = win / (hi - lo).astype(F32) - um_ref[:, cols]
        o_ref[:, cols] = (_bdot(pooled.astype(BF16), pw_ref[g]) * ps_ref[:, cols]).astype(BF16)


def _pool(u, pool_w, pool_scale, bsz):
    t = u.shape[0]
    seq = t // bsz
    tp = 256
    nt = seq // tp
    tiles8 = tp // SUBLANES
    return pl.pallas_call(
        functools.partial(_pool_kernel, seq=seq),
        grid=(bsz, nt),
        in_specs=[
            pl.BlockSpec((tp, POOL_W), lambda b, j: (b * nt + j, 0)),
            pl.BlockSpec((SUBLANES, POOL_W), lambda b, j: (jnp.maximum((b * nt + j) * tiles8 - 1, 0), 0)),
            pl.BlockSpec((SUBLANES, POOL_W),
                         lambda b, j: (jnp.minimum((b * nt + j + 1) * tiles8, t // SUBLANES - 1), 0)),
            pl.BlockSpec((POOL_GROUPS, POOL_GC, POOL_GC), lambda b, j: (0, 0, 0)),
            pl.BlockSpec((1, POOL_W), lambda b, j: (0, 0)),
        ],
        out_specs=pl.BlockSpec((tp, POOL_W), lambda b, j: (b * nt + j, 0)),
        out_shape=jax.ShapeDtypeStruct((t, POOL_W), BF16),
        scratch_shapes=[pltpu.VMEM((tp + 2 * POOL_HALO, POOL_W), F32)],
        compiler_params=_cparams(("parallel", "parallel"), 32),
        name="pool",
    )(u, u, u, pool_w, pool_scale)


def _rope(r, cos, sin_signed):
    lane = lax.broadcasted_iota(jnp.int32, r.shape, 1)
    quarter = MLA_ROPE // 4
    swapped = jnp.where(lane % (2 * quarter) < quarter,
                        pltpu.roll(r, LANES - quarter, 1), pltpu.roll(r, quarter, 1))
    return r * cos + swapped * sin_signed


def _mla_kernel(*refs, use_rope, with_q):
    if use_rope:
        cq_ref, ckv_ref, kr_ref, cos_ref, sin_ref = refs[:5]
        refs = refs[5:]
    else:
        cq_ref, ckv_ref, kr_ref = refs[:3]
        refs = refs[3:]
    qn_ref, wq_ref, kvn_ref, wkv_ref, qhn_ref, khn_ref = refs[:6]
    outs = refs[6:]
    if with_q:
        q_ref, k_ref, v_ref = outs
    else:
        k_ref, v_ref = outs
    rope = (lambda r: _rope(r, cos_ref[...], sin_ref[...])) if use_rope else (lambda r: r)
    inv_qk = 1.0 / MLA_QK

    if with_q:
        qf = _bdot(_rms(cq_ref[...], qn_ref[...]).astype(BF16), wq_ref[...])
        scale = MLA_QK ** -0.5
        for h in range(MLA_HEADS):
            qh = qf[:, h * MLA_QKP:(h + 1) * MLA_QKP]
            rs = lax.rsqrt(jnp.sum(qh * qh, axis=-1, keepdims=True) * inv_qk + NORM_EPS) * scale
            qh = qh * rs * qhn_ref[...]
            q_ref[:, h * MLA_QKP:h * MLA_QKP + MLA_NOPE] = qh[:, :MLA_NOPE].astype(BF16)
            q_ref[:, h * MLA_QKP + MLA_NOPE:(h + 1) * MLA_QKP] = rope(qh[:, MLA_NOPE:]).astype(BF16)

    kvf = _bdot(_rms(ckv_ref[...], kvn_ref[...]).astype(BF16), wkv_ref[...])
    kr = kr_ref[...]
    ss_kr = jnp.sum(kr * kr, axis=-1, keepdims=True)
    kr_w = rope(kr * khn_ref[:, MLA_NOPE:])
    for h in range(MLA_HEADS):
        kn = kvf[:, h * MLA_NOPE:(h + 1) * MLA_NOPE]
        rs = lax.rsqrt((jnp.sum(kn * kn, axis=-1, keepdims=True) + ss_kr) * inv_qk + NORM_EPS)
        k_ref[:, h * MLA_QKP:h * MLA_QKP + MLA_NOPE] = (kn * rs * khn_ref[:, :MLA_NOPE]).astype(BF16)
        k_ref[:, h * MLA_QKP + MLA_NOPE:(h + 1) * MLA_QKP] = (kr_w * rs).astype(BF16)
    v_ref[...] = kvf[:, MLA_HEADS * MLA_NOPE:].astype(BF16)


def _mla_proj(cq, ckv, kr, rope_tabs, qn, wq, kvn, wkv, qhn, khn, with_q):
    t = cq.shape[0]
    tm = 256
    use_rope = rope_tabs is not None
    row = lambda i: (i, 0)
    const = lambda i: (0, 0)
    ins = [cq, ckv, kr]
    in_specs = [pl.BlockSpec((tm, Q_LORA), row), pl.BlockSpec((tm, KV_LORA), row), pl.BlockSpec((tm, LANES), row)]
    if use_rope:
        cos, sin = rope_tabs
        nrt = cos.shape[0] // tm
        ins += [cos, sin]
        in_specs += [pl.BlockSpec((tm, LANES), lambda i: (i % nrt, 0))] * 2
    ins += [qn, wq, kvn, wkv, qhn, khn]
    in_specs += [pl.BlockSpec((1, Q_LORA), const), pl.BlockSpec((Q_LORA, MLA_HEADS * MLA_QKP), const),
                 pl.BlockSpec((1, KV_LORA), const), pl.BlockSpec((KV_LORA, MLA_HEADS * (MLA_NOPE + MLA_V)), const),
                 pl.BlockSpec((1, MLA_QKP), const), pl.BlockSpec((1, MLA_QKP), const)]
    out_specs = [pl.BlockSpec((tm, MLA_HEADS * MLA_QKP), row), pl.BlockSpec((tm, MLA_W), row)]
    out_shape = [jax.ShapeDtypeStruct((t, MLA_HEADS * MLA_QKP), BF16), jax.ShapeDtypeStruct((t, MLA_W), BF16)]
    if with_q:
        out_specs = [pl.BlockSpec((tm, MLA_HEADS * MLA_QKP), row)] + out_specs
        out_shape = [jax.ShapeDtypeStruct((t, MLA_HEADS * MLA_QKP), BF16)] + out_shape
    return pl.pallas_call(
        functools.partial(_mla_kernel, use_rope=use_rope, with_q=with_q),
        grid=(t // tm,),
        in_specs=in_specs,
        out_specs=out_specs,
        out_shape=out_shape,
        compiler_params=_cparams(("parallel",), 40),
        name="mla_proj",
    )(*ins)


def _attn_kernel(*refs, latent_keys):
    if latent_keys:
        q_ref, kc_ref, vc_ref, kl_ref, vl_ref, o_ref = refs
    else:
        q_ref, kc_ref, vc_ref, o_ref = refs
    nt = (((1,), (1,)), ((), ()))
    q = q_ref[...]
    sc = lax.dot_general(q, kc_ref[...], nt, preferred_element_type=F32)
    m = jnp.max(sc, axis=-1, keepdims=True)
    if latent_keys:
        sl = lax.dot_general(q, kl_ref[...], nt, preferred_element_type=F32)
        m = jnp.maximum(m, jnp.max(sl, axis=-1, keepdims=True))
    pc = jnp.exp(sc - m)
    den = jnp.sum(pc, axis=-1, keepdims=True)
    o = _bdot(pc.astype(BF16), vc_ref[...])
    if latent_keys:
        pl_ = jnp.exp(sl - m)
        den = den + jnp.sum(pl_, axis=-1, keepdims=True)
        o = o + _bdot(pl_.astype(BF16), vl_ref[...])
    o_ref[...] = (o / den).astype(BF16)


def _attention(q, kc, vc, kl, vl, bsz):
    tq_total = q.shape[0]
    lq = tq_total // bsz
    lc = kc.shape[0] // bsz
    tq = min(512, lq)
    nq = lq // tq
    latent_keys = kl is not None
    ins = [q, kc, vc]
    in_specs = [pl.BlockSpec((tq, MLA_QKP), lambda b, h, i: (b * nq + i, h)),
                pl.BlockSpec((lc, MLA_QKP), lambda b, h, i: (b, h)),
                pl.BlockSpec((lc, MLA_V), lambda b, h, i: (b, h))]
    if latent_keys:
        ll = kl.shape[0] // bsz
        ins += [kl, vl]
        in_specs += [pl.BlockSpec((ll, MLA_QKP), lambda b, h, i: (b, h)),
                     pl.BlockSpec((ll, MLA_V), lambda b, h, i: (b, h))]
    return pl.pallas_call(
        functools.partial(_attn_kernel, latent_keys=latent_keys),
        grid=(bsz, MLA_HEADS, nq),
        in_specs=in_specs,
        out_specs=pl.BlockSpec((tq, MLA_V), lambda b, h, i: (b * nq + i, h)),
        out_shape=jax.ShapeDtypeStruct((tq_total, MLA_W), BF16),
        compiler_params=_cparams(("parallel", "parallel", "arbitrary"), 48),
        name="attention",
    )(*ins)


def _out_proj_kernel(*refs, route):
    ssd_ref, pool_ref, att_ref, h_ref, mod_ref, nw_ref, w_ref = refs[:7]
    refs = refs[7:]
    if route:
        wr_hi_ref, wr_lo_ref, ho_ref, v_ref, r_ref = refs
    else:
        ho_ref, v_ref = refs
    mod = mod_ref[0]
    mix = (_bdot(ssd_ref[...], w_ref[0:SSD_W, :])
           + _bdot(pool_ref[...], w_ref[SSD_W:SSD_W + POOL_W, :])
           + _bdot(att_ref[...], w_ref[SSD_W + POOL_W:, :]))
    hn = h_ref[...] + mod[2:3, :] * mix
    ho_ref[...] = hn
    v = _rms(hn, nw_ref[...]) * (1.0 + mod[4:5, :]) + mod[3:4, :]
    vb = v.astype(BF16)
    v_ref[...] = vb
    if route:
        v_lo = (v - vb.astype(F32)).astype(BF16)
        logits = _bdot(vb, wr_hi_ref[...]) + _bdot(v_lo, wr_hi_ref[...]) + _bdot(vb, wr_lo_ref[...])
        lane = lax.broadcasted_iota(jnp.int32, logits.shape, 1).astype(F32)
        neg = -jnp.inf
        logits = jnp.where(lane < N_EXPERTS, logits, neg)
        m1 = jnp.max(logits, axis=-1, keepdims=True)
        i1 = jnp.min(jnp.where(logits == m1, lane, float(LANES)), axis=-1, keepdims=True)
        rest = jnp.where(lane == i1, neg, logits)
        m2 = jnp.max(rest, axis=-1, keepdims=True)
        i2 = jnp.min(jnp.where(rest == m2, lane, float(LANES)), axis=-1, keepdims=True)
        e2 = jnp.exp(m2 - m1)
        g1 = 1.0 / (1.0 + e2)
        g2 = e2 / (1.0 + e2)
        r_ref[...] = jnp.where(lane == 0, i1, jnp.where(lane == 1, i2, jnp.where(lane == 2, g1, jnp.where(lane == 3, g2, 0.0))))


def _out_proj(ssd, pool, att, h, mod, norm_w, w, rows_per_mod, router=None):
    t, d = h.shape
    tm = 256
    tiles_per_mod = rows_per_mod // tm
    row = lambda i: (i, 0)
    const = lambda i: (0, 0)
    route = router is not None
    ins = [ssd, pool, att, h, mod, norm_w, w]
    in_specs = [pl.BlockSpec((tm, SSD_W), row), pl.BlockSpec((tm, POOL_W), row), pl.BlockSpec((tm, MLA_W), row),
                pl.BlockSpec((tm, d), row), pl.BlockSpec((1, 6, d), lambda i: (i // tiles_per_mod, 0, 0)),
                pl.BlockSpec((1, d), const), pl.BlockSpec((d, d), const)]
    out_specs = [pl.BlockSpec((tm, d), row), pl.BlockSpec((tm, d), row)]
    out_shape = [jax.ShapeDtypeStruct((t, d), F32), jax.ShapeDtypeStruct((t, d), BF16)]
    if route:
        ins += list(router)
        in_specs += [pl.BlockSpec((d, LANES), const)] * 2
        out_specs.append(pl.BlockSpec((tm, LANES), row))
        out_shape.append(jax.ShapeDtypeStruct((t, LANES), F32))
    return pl.pallas_call(
        functools.partial(_out_proj_kernel, route=route),
        grid=(t // tm,),
        in_specs=in_specs,
        out_specs=out_specs,
        out_shape=out_shape,
        compiler_params=_cparams(("parallel",), 48),
        name="out_proj",
    )(*ins)


def _ffn_kernel(v_ref, wg_ref, wu_ref, wd_ref, h_ref, mod_ref, o_ref, acc_scr):
    f = pl.program_id(1)

    @pl.when(f == 0)
    def _():
        acc_scr[...] = jnp.zeros_like(acc_scr)

    v = v_ref[...]
    act = (_silu(_bdot(v, wg_ref[...])) * _bdot(v, wu_ref[...])).astype(BF16)
    acc_scr[...] += _bdot(act, wd_ref[...])

    @pl.when(f == pl.num_programs(1) - 1)
    def _():
        o_ref[...] = h_ref[...] + mod_ref[0][5:6, :] * acc_scr[...]


def _ffn(v, wg, wu, wd, h, mod, rows_per_mod):
    t, d = h.shape
    tm = 512
    tf = 512
    tiles_per_mod = rows_per_mod // tm
    nf = wg.shape[1] // tf
    return pl.pallas_call(
        _ffn_kernel,
        grid=(t // tm, nf),
        in_specs=[pl.BlockSpec((tm, d), lambda i, f: (i, 0)),
                  pl.BlockSpec((d, tf), lambda i, f: (0, f)),
                  pl.BlockSpec((d, tf), lambda i, f: (0, f)),
                  pl.BlockSpec((tf, d), lambda i, f: (f, 0)),
                  pl.BlockSpec((tm, d), lambda i, f: (i, 0)),
                  pl.BlockSpec((1, 6, d), lambda i, f: (i // tiles_per_mod, 0, 0))],
        out_specs=pl.BlockSpec((tm, d), lambda i, f: (i, 0)),
        out_shape=jax.ShapeDtypeStruct((t, d), F32),
        scratch_shapes=[pltpu.VMEM((tm, d), F32)],
        compiler_params=_cparams(("parallel", "arbitrary"), 48),
        name="ffn",
    )(v, wg, wu, wd, h, mod)


MOE_TM = 512
MOE_TF = 512


def _moe_kernel(te_ref, tv_ref, x_ref, wg_ref, wu_ref, wd_ref, o_ref, acc_scr):
    i = pl.program_id(0)
    f = pl.program_id(1)

    @pl.when(tv_ref[i] == 1)
    def _():
        @pl.when(f == 0)
        def _():
            acc_scr[...] = jnp.zeros_like(acc_scr)

        x = x_ref[...]
        act = (_silu(_bdot(x, wg_ref[0])) * _bdot(x, wu_ref[0])).astype(BF16)
        acc_scr[...] += _bdot(act, wd_ref[0])

        @pl.when(f == pl.num_programs(1) - 1)
        def _():
            o_ref[...] = acc_scr[...]


def _moe_experts(tile_expert, tile_valid, x_sorted, wg, wu, wd):
    r, d = x_sorted.shape
    tm, tf = MOE_TM, MOE_TF
    nf = wg.shape[2] // tf

    def fidx(i, f, tv):
        return jnp.where(tv[i] == 1, f, nf - 1)

    grid_spec = pltpu.PrefetchScalarGridSpec(
        num_scalar_prefetch=2,
        grid=(r // tm, nf),
        in_specs=[pl.BlockSpec((tm, d), lambda i, f, te, tv: (i, 0)),
                  pl.BlockSpec((1, d, tf), lambda i, f, te, tv: (te[i], 0, fidx(i, f, tv))),
                  pl.BlockSpec((1, d, tf), lambda i, f, te, tv: (te[i], 0, fidx(i, f, tv))),
                  pl.BlockSpec((1, tf, d), lambda i, f, te, tv: (te[i], fidx(i, f, tv), 0))],
        out_specs=pl.BlockSpec((tm, d), lambda i, f, te, tv: (i, 0)),
        scratch_shapes=[pltpu.VMEM((tm, d), F32)],
    )
    return pl.pallas_call(
        _moe_kernel,
        grid_spec=grid_spec,
        out_shape=jax.ShapeDtypeStruct((r, d), F32),
        compiler_params=_cparams(("arbitrary", "arbitrary"), 48),
        name="moe_experts",
    )(tile_expert, tile_valid, x_sorted, wg, wu, wd)


def _combine_kernel(h_ref, y0_ref, y1_ref, r_ref, mod_ref, o_ref):
    r = r_ref[...]
    mixed = r[:, 2:3] * y0_ref[...] + r[:, 3:4] * y1_ref[...]
    o_ref[...] = h_ref[...] + mod_ref[0][5:6, :] * mixed


def _combine(h, y0, y1, route, mod, rows_per_mod):
    t, d = h.shape
    tm = 512
    tiles_per_mod = rows_per_mod // tm
    row = lambda i: (i, 0)
    return pl.pallas_call(
        _combine_kernel,
        grid=(t // tm,),
        in_specs=[pl.BlockSpec((tm, d), row), pl.BlockSpec((tm, d), row), pl.BlockSpec((tm, d), row),
                  pl.BlockSpec((tm, LANES), row),
                  pl.BlockSpec((1, 6, d), lambda i: (i // tiles_per_mod, 0, 0))],
        out_specs=pl.BlockSpec((tm, d), row),
        out_shape=jax.ShapeDtypeStruct((t, d), F32),
        compiler_params=_cparams(("parallel",), 48),
        name="moe_combine",
    )(h, y0, y1, route, mod)


def _moe(v, route, h, mod, wg, wu, wd, rows_per_mod):
    t = v.shape[0]
    tm = MOE_TM
    n_assign = t * TOP_K
    rows = n_assign + N_EXPERTS * tm
    expert = route[:, :TOP_K].astype(jnp.int32).reshape(-1)
    onehot = (expert[:, None] == jnp.arange(N_EXPERTS, dtype=jnp.int32)[None, :]).astype(jnp.int32)
    csum = jnp.cumsum(onehot, axis=0)
    rank = jnp.sum(csum * onehot, axis=1) - 1
    counts = csum[-1]
    padded = ((counts + tm - 1) // tm) * tm
    ends = jnp.cumsum(padded)
    dest = (ends - padded)[expert] + rank
    row_token = jnp.zeros((rows,), jnp.int32).at[dest].set(jnp.arange(n_assign, dtype=jnp.int32) // TOP_K)
    tile_start = jnp.arange(rows // tm, dtype=jnp.int32) * tm
    tile_valid = (tile_start < ends[-1]).astype(jnp.int32)
    tile_expert = jnp.minimum(jnp.searchsorted(ends, tile_start, side="right"), N_EXPERTS - 1).astype(jnp.int32)
    last_valid = jnp.maximum(ends[-1] // tm - 1, 0)
    tile_expert = jnp.where(tile_valid == 1, tile_expert, tile_expert[last_valid])
    x_sorted = jnp.take(v, row_token, axis=0)
    y_sorted = _moe_experts(tile_expert, tile_valid, x_sorted, wg, wu, wd)
    dest2 = dest.reshape(t, TOP_K)
    y0 = jnp.take(y_sorted, dest2[:, 0], axis=0)
    y1 = jnp.take(y_sorted, dest2[:, 1], axis=0)
    return _combine(h, y0, y1, route, mod, rows_per_mod)


def _pad_cols(a, n):
    return jnp.pad(a, ((0, 0), (0, n - a.shape[1])))


def _prep_w_in(w):
    z, xbc, dtr, pool, cq, ckv, kr = jnp.split(w, np.cumsum(IN_SIZES)[:-1].tolist(), axis=1)
    parts = [z, xbc, pool, cq, ckv, _pad_cols(kr, LANES), _pad_cols(dtr[:, :SSD_HEADS], LANES),
             _pad_cols(dtr[:, SSD_HEADS:], LANES)]
    return jnp.concatenate(parts, axis=1).astype(BF16)


def _prep_w_uq(w):
    w = w.reshape(Q_LORA, MLA_HEADS, MLA_QK)
    w = jnp.pad(w, ((0, 0), (0, 0), (0, MLA_QKP - MLA_QK)))
    return w.reshape(Q_LORA, MLA_HEADS * MLA_QKP).astype(BF16)


def _prep_w_ukv(w):
    w = w.reshape(KV_LORA, MLA_HEADS, MLA_NOPE + MLA_V)
    k = w[:, :, :MLA_NOPE].reshape(KV_LORA, MLA_HEADS * MLA_NOPE)
    v = w[:, :, MLA_NOPE:].reshape(KV_LORA, MLA_HEADS * MLA_V)
    return jnp.concatenate([k, v], axis=1).astype(BF16)


def _rope_tables(seq):
    quarter = MLA_ROPE // 4
    inv = ROPE_BASE ** (-jnp.arange(quarter, dtype=F32) / quarter)
    pos = jnp.arange(seq, dtype=jnp.int32)
    pos_row = (pos // GRID_W).astype(F32)
    pos_col = (pos % GRID_W).astype(F32)
    ang_row = pos_row[:, None] * inv[None, :]
    ang_col = pos_col[:, None] * inv[None, :]
    ang = jnp.concatenate([ang_row, ang_row, ang_col, ang_col], axis=1)
    sign = jnp.tile(jnp.concatenate([-jnp.ones((quarter,), F32), jnp.ones((quarter,), F32)]), 2)
    cos = jnp.concatenate([jnp.cos(ang), jnp.ones((seq, LANES - MLA_ROPE), F32)], axis=1)
    sin = jnp.concatenate([jnp.sin(ang) * sign[None, :], jnp.zeros((seq, LANES - MLA_ROPE), F32)], axis=1)
    return cos, sin


def _head_expand():
    k = np.arange(LANES)[:, None]
    j = np.arange(SSD_W)[None, :]
    return jnp.asarray((k == j // SSD_HEADDIM).astype(np.float32), dtype=BF16)


def _split_hi_lo(w):
    hi = w.astype(BF16)
    lo = (w - hi.astype(F32)).astype(BF16)
    return hi, lo


def kernel(x, c, ctx, c_ctx, ada_w, ada_b, norm_mix_w, norm_ffn_w, w_in, ssd_conv_w, ssd_conv_b, ssd_dt_bias, ssd_a_log, ssd_d, ssd_norm_w, pool_w, pool_scale, mla_q_norm_w, mla_w_uq, mla_kv_norm_w, mla_w_ukv, mla_q_head_norm_w, mla_k_head_norm_w, w_out, ffn_w_gate, ffn_w_up, ffn_w_down, moe_w_router, moe_w_gate, moe_w_up, moe_w_down):
    bsz, seq, d = x.shape
    lc = ctx.shape[1]
    depth = ada_w.shape[0]
    h_l = x.reshape(bsz * seq, d)
    h_c = ctx.reshape(bsz * lc, d)
    t_l, t_c = h_l.shape[0], h_c.shape[0]

    pad_rows = (-(bsz + 1)) % SUBLANES
    cvec = jnp.concatenate([c, c_ctx[None, :], jnp.zeros((pad_rows, d), F32)], axis=0)
    rope_tabs = _rope_tables(seq)
    expand = _head_expand()
    zero_state = jnp.zeros((bsz, 2, SSD_GROUPS, SSD_STATE, SSD_GW), F32)

    for i in range(depth):
        last = i == depth - 1
        j = i // 2
        mod = _ada(cvec, ada_w[i], ada_b[i][None, :])
        mod_l = mod[:bsz].reshape(bsz, 6, d)
        mod_c = mod[bsz:bsz + 1].reshape(1, 6, d)

        w_in_i = _prep_w_in(w_in[i])
        nmw = norm_mix_w[i][None, :]
        z_l, xbc_l, pool_l, cq_l, ckv_l, kr_l, dt_l = _in_proj(h_l, mod_l, nmw, w_in_i, seq)
        z_c, xbc_c, pool_c, cq_c, ckv_c, kr_c, dt_c = _in_proj(h_c, mod_c, nmw, w_in_i, t_c)

        conv_b = ssd_conv_b[i][None, :]
        dtb = _pad_cols(ssd_dt_bias[i], LANES)[:, None, :]
        alog = _pad_cols(ssd_a_log[i], LANES)[:, None, :]
        dsk = jnp.repeat(ssd_d[i], SSD_HEADDIM)[None, :]
        snw = ssd_norm_w[i][None, :]
        ssd_args = (ssd_conv_w[i], conv_b, dtb, alog, dsk, snw, expand, bsz)
        ssd_c, st_c = _ssd(xbc_c, z_c, dt_c, zero_state, *ssd_args)
        ssd_l, _ = _ssd(xbc_l, z_l, dt_l, st_c, *ssd_args)

        pw = pool_w[i].astype(BF16)
        psc = pool_scale[i][None, :]
        pool_out_l = _pool(pool_l, pw, psc, bsz)

        qhn = _pad_cols(mla_q_head_norm_w[i][None, :], MLA_QKP)
        khn = _pad_cols(mla_k_head_norm_w[i][None, :], MLA_QKP)
        mla_args = (mla_q_norm_w[i][None, :], _prep_w_uq(mla_w_uq[i]), mla_kv_norm_w[i][None, :],
                    _prep_w_ukv(mla_w_ukv[i]), qhn, khn)
        q_l, k_l, v_l = _mla_proj(cq_l, ckv_l, kr_l, rope_tabs, *mla_args, with_q=True)
        if last:
            k_c, v_c = _mla_proj(cq_c, ckv_c, kr_c, None, *mla_args, with_q=False)
        else:
            q_c, k_c, v_c = _mla_proj(cq_c, ckv_c, kr_c, None, *mla_args, with_q=True)
        att_l = _attention(q_l, k_c, v_c, k_l, v_l, bsz)

        w_out_i = w_out[i].astype(BF16)
        nfw = norm_ffn_w[i][None, :]
        dense = i % 2 == 0
        router = None if dense else _split_hi_lo(_pad_cols(moe_w_router[j], LANES))
        res_l = _out_proj(ssd_l, pool_out_l, att_l, h_l, mod_l, nfw, w_out_i, seq, router)
        if dense:
            wg, wu, wd = ffn_w_gate[j].astype(BF16), ffn_w_up[j].astype(BF16), ffn_w_down[j].astype(BF16)
            h_l = _ffn(res_l[1], wg, wu, wd, res_l[0], mod_l, seq)
        else:
            wg, wu, wd = moe_w_gate[j].astype(BF16), moe_w_up[j].astype(BF16), moe_w_down[j].astype(BF16)
            h_l = _moe(res_l[1], res_l[2], res_l[0], mod_l, wg, wu, wd, seq)
        if not last:
            pool_out_c = _pool(pool_c, pw, psc, bsz)
            att_c = _attention(q_c, k_c, v_c, None, None, bsz)
            res_c = _out_proj(ssd_c, pool_out_c, att_c, h_c, mod_c, nfw, w_out_i, t_c, router)
            if dense:
                h_c = _ffn(res_c[1], wg, wu, wd, res_c[0], mod_c, t_c)
            else:
                h_c = _moe(res_c[1], res_c[2], res_c[0], mod_c, wg, wu, wd, t_c)
    return h_l.reshape(bsz, seq, d)
```

```python
import functools

import numpy as np
import jax
import jax.numpy as jnp
from jax import lax
from jax.experimental import pallas as pl
from jax.experimental.pallas import tpu as pltpu

F32 = jnp.float32
BF16 = jnp.bfloat16

D_MODEL = 2048
GRID_W = 64
SSD_HEADS = 12
SSD_HEADDIM = 64
SSD_W = SSD_HEADS * SSD_HEADDIM
SSD_GROUPS = 2
SSD_HPG = SSD_HEADS // SSD_GROUPS
SSD_STATE = 128
SSD_CONV = 4
SSD_CHUNK = 128
SSD_BC = SSD_GROUPS * SSD_STATE
SSD_CONV_CH = SSD_W + 2 * SSD_BC
SSD_GW = SSD_HPG * SSD_HEADDIM
POOL_WINDOWS = (2, 4, 8, 16)
POOL_GROUPS = 4
POOL_GC = 128
POOL_W = POOL_GROUPS * POOL_GC
POOL_HALO = 8
MLA_HEADS = 6
MLA_NOPE = 128
MLA_ROPE = 64
MLA_QK = MLA_NOPE + MLA_ROPE
MLA_QKP = 256
MLA_V = 128
MLA_W = MLA_HEADS * MLA_V
Q_LORA = 512
KV_LORA = 512
ROPE_BASE = 10000.0
LOG2_E = 1.4426950408889634
FFN_DIM = 5632
N_EXPERTS = 8
TOP_K = 2
NORM_EPS = 1e-6
IN_SIZES = (SSD_W, SSD_CONV_CH, 2 * SSD_HEADS, POOL_W, Q_LORA, KV_LORA, MLA_ROPE)
LANES = 128
SUBLANES = 8
IN_Z = 0
IN_XBC = IN_Z + SSD_W
IN_POOL = IN_XBC + SSD_CONV_CH
IN_CQ = IN_POOL + POOL_W
IN_CKV = IN_CQ + Q_LORA
IN_KR = IN_CKV + KV_LORA
IN_DT0 = IN_KR + LANES
IN_DT1 = IN_DT0 + LANES
IN_COLS = IN_DT1 + LANES
MIB = 1024 * 1024


def _cparams(sem, vmem_mib):
    return pltpu.CompilerParams(dimension_semantics=sem, vmem_limit_bytes=vmem_mib * MIB)


def _sigmoid(x):
    return 1.0 / (1.0 + jnp.exp(-x))


def _silu(x):
    return x * _sigmoid(x)


def _softplus(x):
    return jnp.maximum(x, 0.0) + jnp.log1p(jnp.exp(-jnp.abs(x)))


def _rms(x, w):
    return x * lax.rsqrt(jnp.mean(x * x, axis=-1, keepdims=True) + NORM_EPS) * w


def _bdot(a, b):
    return jnp.dot(a, b, preferred_element_type=F32)


def _split_bf16(x, passes):
    parts = []
    r = x
    for _ in range(passes):
        p = r.astype(BF16)
        parts.append(p)
        r = r - p.astype(F32)
    return parts


def _ldot(m_bf16, x, passes):
    out = None
    for p in _split_bf16(x, passes):
        t = _bdot(m_bf16, p)
        out = t if out is None else out + t
    return out


def _rdot(x, m_bf16, passes):
    out = None
    for p in _split_bf16(x, passes):
        t = _bdot(p, m_bf16)
        out = t if out is None else out + t
    return out


def _ada_kernel(c_ref, w_ref, b_ref, o_ref):
    s = _silu(c_ref[...]).astype(BF16)
    o_ref[...] = _bdot(s, w_ref[...].astype(BF16)) + b_ref[...]


def _ada(cvec, w, b):
    rows, d = cvec.shape
    n = w.shape[1]
    tn = 1024
    return pl.pallas_call(
        _ada_kernel,
        grid=(n // tn,),
        in_specs=[pl.BlockSpec((rows, d), lambda j: (0, 0)),
                  pl.BlockSpec((d, tn), lambda j: (0, j)),
                  pl.BlockSpec((1, tn), lambda j: (0, j))],
        out_specs=pl.BlockSpec((rows, tn), lambda j: (0, j)),
        out_shape=jax.ShapeDtypeStruct((rows, n), F32),
        compiler_params=_cparams(("arbitrary",), 40),
        name="ada",
    )(cvec, w, b)


def _in_proj_kernel(h_ref, mod_ref, nw_ref, w_ref, z_ref, xbc_ref, pool_ref, cq_ref, ckv_ref,
                    kr_ref, dt_ref):
    mod = mod_ref[0]
    u = _rms(h_ref[...], nw_ref[...]) * (1.0 + mod[1:2, :]) + mod[0:1, :]
    ub = u.astype(BF16)
    z_ref[...] = _bdot(ub, w_ref[:, IN_Z:IN_XBC])
    xbc_ref[...] = _bdot(ub, w_ref[:, IN_XBC:IN_POOL])
    pool_ref[...] = _bdot(ub, w_ref[:, IN_POOL:IN_CQ])
    cq_ref[...] = _bdot(ub, w_ref[:, IN_CQ:IN_CKV])
    ckv_ref[...] = _bdot(ub, w_ref[:, IN_CKV:IN_KR])
    kr_ref[...] = _bdot(ub, w_ref[:, IN_KR:IN_DT0])
    dt_ref[0] = _bdot(ub, w_ref[:, IN_DT0:IN_DT1])
    dt_ref[1] = _bdot(ub, w_ref[:, IN_DT1:IN_COLS])


def _in_proj(h, mod, norm_w, w, rows_per_mod):
    t, d = h.shape
    tm = 256
    tiles_per_mod = rows_per_mod // tm
    widths = (SSD_W, SSD_CONV_CH, POOL_W, Q_LORA, KV_LORA, LANES)
    row = lambda i: (i, 0)
    return pl.pallas_call(
        _in_proj_kernel,
        grid=(t // tm,),
        in_specs=[pl.BlockSpec((tm, d), row),
                  pl.BlockSpec((1, 6, d), lambda i: (i // tiles_per_mod, 0, 0)),
                  pl.BlockSpec((1, d), lambda i: (0, 0)),
                  pl.BlockSpec((d, IN_COLS), lambda i: (0, 0))],
        out_specs=[pl.BlockSpec((tm, wd), row) for wd in widths]
        + [pl.BlockSpec((2, tm, LANES), lambda i: (0, i, 0))],
        out_shape=[jax.ShapeDtypeStruct((t, wd), F32) for wd in widths]
        + [jax.ShapeDtypeStruct((2, t, LANES), F32)],
        compiler_params=_cparams(("parallel",), 56),
        name="in_proj",
    )(h, mod, norm_w, w)


def _ssd_kernel(xm_ref, xp_ref, xn_ref, z_ref, dt_ref, init_ref, cw_ref, cb_ref, dtb_ref, alog_ref,
                dsk_ref, nw_ref, e_ref, y_ref, fin_ref, yf_scr, st_scr, xe_scr, *, nc):
    q = SSD_CHUNK
    t = pl.program_id(1)
    d = t // nc
    cpos = t % nc
    c = jnp.where(d == 0, cpos, nc - 1 - cpos)

    @pl.when(cpos == 0)
    def _():
        st_scr[...] = init_ref[0, 0]

    xe_scr[0:SUBLANES, :] = jnp.where(c == 0, 0.0, xp_ref[...])
    xe_scr[SUBLANES:SUBLANES + q, :] = xm_ref[...]
    xe_scr[SUBLANES + q:, :] = jnp.where(c == nc - 1, 0.0, xn_ref[...])
    cw = cw_ref[...]
    off = SUBLANES - SSD_CONV // 2
    acc = cb_ref[...]
    for k in range(SSD_CONV):
        acc = acc + cw[k:k + 1, :] * xe_scr[off + k:off + k + q, :]
    xbc = _silu(acc)
    xs = xbc[:, :SSD_W]

    lane = lax.broadcasted_iota(jnp.int32, (q, LANES), 1)
    row = lax.broadcasted_iota(jnp.int32, (q, LANES), 0)
    dt = jnp.where(lane < SSD_HEADS, _softplus(dt_ref[0] + dtb_ref[0]), 0.0)
    adt = dt * (-jnp.exp(alog_ref[0]))
    causal = (row - lane) * (1 - 2 * d) >= 0
    tri = jnp.where(causal, 1.0, 0.0).astype(BF16)
    cs = _ldot(tri, adt, 3)
    tot = jnp.sum(adt, axis=0, keepdims=True)
    e = e_ref[...]
    dtx = _rdot(dt, e, 2)
    ecsx = _rdot(jnp.exp(cs), e, 2)
    wx = _rdot(dt * jnp.exp(tot - cs), e, 2)
    etot = _rdot(jnp.broadcast_to(jnp.exp(tot), (SUBLANES, LANES)), e, 2)[0:1, :]
    xdt = xs * dtx
    xw = (xs * wx).astype(BF16)
    cs_t = cs.T
    low_half = lane < SSD_HEADDIM

    ys = []
    for g in range(SSD_GROUPS):
        bg = xbc[:, SSD_W + g * SSD_STATE:SSD_W + (g + 1) * SSD_STATE]
        cg = xbc[:, SSD_W + SSD_BC + g * SSD_STATE:SSD_W + SSD_BC + (g + 1) * SSD_STATE]
        cgb = cg.astype(BF16)
        cb = lax.dot_general(cgb, bg.astype(BF16), (((1,), (1,)), ((), ())),
                             preferred_element_type=F32)
        gs = slice(g * SSD_GW, (g + 1) * SSD_GW)
        st = st_scr[g]
        y_off = _bdot(cgb, st.astype(BF16)) * ecsx[:, gs]
        st_scr[g] = st * etot[:, gs] + _bdot(bg.T.astype(BF16), xw[:, gs])
        for pr in range(SSD_HPG // 2):
            xpair = xdt[:, g * SSD_GW + pr * LANES:g * SSD_GW + (pr + 1) * LANES]
            y_pair = y_off[:, pr * LANES:(pr + 1) * LANES]
            for k in range(2):
                h = g * SSD_HPG + pr * 2 + k
                dcs = cs[:, h:h + 1] - cs_t[h:h + 1, :]
                m = (cb * jnp.exp(jnp.where(causal, dcs, -1e30))).astype(BF16)
                xh = jnp.where(low_half if k == 0 else jnp.logical_not(low_half), xpair, 0.0)
                y_pair = y_pair + _bdot(m, xh.astype(BF16))
            ys.append(y_pair)
    y = jnp.concatenate(ys, axis=1)

    r0 = pl.multiple_of(c * q, q)

    @pl.when(d == 0)
    def _():
        yf_scr[pl.ds(r0, q), :] = y

    @pl.when(d == 1)
    def _():
        g_ = (yf_scr[pl.ds(r0, q), :] + y + xs * dsk_ref[...]) * _silu(z_ref[...])
        y_ref[...] = _rms(g_, nw_ref[...]).astype(BF16)

    @pl.when(cpos == nc - 1)
    def _():
        fin_ref[0, 0] = st_scr[...]


def _ssd(xbc, z, dt, init, conv_w, conv_b, dt_bias, a_log, d_skip, norm_w, expand, bsz):
    t = xbc.shape[0]
    q = SSD_CHUNK
    nc = t // bsz // q
    tiles8 = q // SUBLANES

    def chunk(tt):
        cp = tt % nc
        return jnp.where(tt < nc, cp, nc - 1 - cp)

    def out_chunk(tt):
        return jnp.where(tt < nc, nc - 1, 2 * nc - 1 - tt)

    const2 = lambda b, tt: (0, 0)
    state_spec = pl.BlockSpec((1, 1, SSD_GROUPS, SSD_STATE, SSD_GW), lambda b, tt: (b, tt // nc, 0, 0, 0))
    state_shape = jax.ShapeDtypeStruct((bsz, 2, SSD_GROUPS, SSD_STATE, SSD_GW), F32)
    return pl.pallas_call(
        functools.partial(_ssd_kernel, nc=nc),
        grid=(bsz, 2 * nc),
        in_specs=[
            pl.BlockSpec((q, SSD_CONV_CH), lambda b, tt: (b * nc + chunk(tt), 0)),
            pl.BlockSpec((SUBLANES, SSD_CONV_CH),
                         lambda b, tt: (jnp.maximum((b * nc + chunk(tt)) * tiles8 - 1, 0), 0)),
            pl.BlockSpec((SUBLANES, SSD_CONV_CH),
                         lambda b, tt: (jnp.minimum((b * nc + chunk(tt) + 1) * tiles8, t // SUBLANES - 1), 0)),
            pl.BlockSpec((q, SSD_W), lambda b, tt: (b * nc + out_chunk(tt), 0)),
            pl.BlockSpec((1, q, LANES), lambda b, tt: (tt // nc, b * nc + chunk(tt), 0)),
            state_spec,
            pl.BlockSpec((SSD_CONV, SSD_CONV_CH), const2),
            pl.BlockSpec((1, SSD_CONV_CH), const2),
            pl.BlockSpec((1, 1, LANES), lambda b, tt: (tt // nc, 0, 0)),
            pl.BlockSpec((1, 1, LANES), lambda b, tt: (tt // nc, 0, 0)),
            pl.BlockSpec((1, SSD_W), const2),
            pl.BlockSpec((1, SSD_W), const2),
            pl.BlockSpec((LANES, SSD_W), const2),
        ],
        out_specs=[pl.BlockSpec((q, SSD_W), lambda b, tt: (b * nc + out_chunk(tt), 0)), state_spec],
        out_shape=[jax.ShapeDtypeStruct((t, SSD_W), BF16), state_shape],
        scratch_shapes=[pltpu.VMEM((nc * q, SSD_W), F32),
                        pltpu.VMEM((SSD_GROUPS, SSD_STATE, SSD_GW), F32),
                        pltpu.VMEM((q + 2 * SUBLANES, SSD_CONV_CH), F32)],
        compiler_params=_cparams(("parallel", "arbitrary"), 40),
        name="ssd",
    )(xbc, xbc, xbc, z, dt, init, conv_w, conv_b, dt_bias, a_log, d_skip, norm_w, expand)


def _pool_kernel(um_ref, up_ref, un_ref, pw_ref, ps_ref, o_ref, pe_scr, *, seq):
    tp = um_ref.shape[0]
    j = pl.program_id(1)
    pe_scr[0:POOL_HALO, :] = jnp.where(j == 0, 0.0, up_ref[...])
    pe_scr[POOL_HALO:POOL_HALO + tp, :] = um_ref[...]
    pe_scr[POOL_HALO + tp:, :] = jnp.where(j == pl.num_programs(1) - 1, 0.0, un_ref[...])
    pos = j * tp + lax.broadcasted_iota(jnp.int32, (tp, 1), 0)
    for g, w in enumerate(POOL_WINDOWS):
        cols = slice(g * POOL_GC, (g + 1) * POOL_GC)
        start = POOL_HALO - w // 2
        win = pe_scr[start:start + tp, cols]
        for k in range(1, w):
            win = win + pe_scr[start + k:start + k + tp, cols]
        lo = jnp.maximum(pos - w // 2, 0)
        hi = jnp.minimum(pos - w // 2 + w, seq)
        pooled = win / (hi - lo).astype(F32) - um_ref[:, cols]
        o_ref[:, cols] = (_bdot(pooled.astype(BF16), pw_ref[g]) * ps_ref[:, cols]).astype(BF16)


def _pool(u, pool_w, pool_scale, bsz):
    t = u.shape[0]
    seq = t // bsz
    tp = 256
    nt = seq // tp
    tiles8 = tp // SUBLANES
    return pl.pallas_call(
        functools.partial(_pool_kernel, seq=seq),
        grid=(bsz, nt),
        in_specs=[
            pl.BlockSpec((tp, POOL_W), lambda b, j: (b * nt + j, 0)),
            pl.BlockSpec((SUBLANES, POOL_W), lambda b, j: (jnp.maximum((b * nt + j) * tiles8 - 1, 0), 0)),
            pl.BlockSpec((SUBLANES, POOL_W),
                         lambda b, j: (jnp.minimum((b * nt + j + 1) * tiles8, t // SUBLANES - 1), 0)),
            pl.BlockSpec((POOL_GROUPS, POOL_GC, POOL_GC), lambda b, j: (0, 0, 0)),
            pl.BlockSpec((1, POOL_W), lambda b, j: (0, 0)),
        ],
        out_specs=pl.BlockSpec((tp, POOL_W), lambda b, j: (b * nt + j, 0)),
        out_shape=jax.ShapeDtypeStruct((t, POOL_W), BF16),
        scratch_shapes=[pltpu.VMEM((tp + 2 * POOL_HALO, POOL_W), F32)],
        compiler_params=_cparams(("parallel", "parallel"), 32),
        name="pool",
    )(u, u, u, pool_w, pool_scale)


def _rope(r, cos, sin_signed):
    lane = lax.broadcasted_iota(jnp.int32, r.shape, 1)
    quarter = MLA_ROPE // 4
    swapped = jnp.where(lane % (2 * quarter) < quarter,
                        pltpu.roll(r, LANES - quarter, 1), pltpu.roll(r, quarter, 1))
    return r * cos + swapped * sin_signed


def _mla_kernel(*refs, use_rope, with_q):
    if use_rope:
        cq_ref, ckv_ref, kr_ref, cos_ref, sin_ref = refs[:5]
        refs = refs[5:]
    else:
        cq_ref, ckv_ref, kr_ref = refs[:3]
        refs = refs[3:]
    qn_ref, wq_ref, kvn_ref, wkv_ref, qhn_ref, khn_ref = refs[:6]
    outs = refs[6:]
    if with_q:
        q_ref, k_ref, v_ref = outs
    else:
        k_ref, v_ref = outs
    rope = (lambda r: _rope(r, cos_ref[...], sin_ref[...])) if use_rope else (lambda r: r)
    inv_qk = 1.0 / MLA_QK

    if with_q:
        qf = _bdot(_rms(cq_ref[...], qn_ref[...]).astype(BF16), wq_ref[...])
        scale = MLA_QK ** -0.5 * LOG2_E
        for h in range(MLA_HEADS):
            qh = qf[:, h * MLA_QKP:(h + 1) * MLA_QKP]
            rs = lax.rsqrt(jnp.sum(qh * qh, axis=-1, keepdims=True) * inv_qk + NORM_EPS) * scale
            qh = qh * rs * qhn_ref[...]
            q_ref[:, h * MLA_QKP:h * MLA_QKP + MLA_NOPE] = qh[:, :MLA_NOPE].astype(BF16)
            q_ref[:, h * MLA_QKP + MLA_NOPE:(h + 1) * MLA_QKP] = rope(qh[:, MLA_NOPE:]).astype(BF16)

    kvf = _bdot(_rms(ckv_ref[...], kvn_ref[...]).astype(BF16), wkv_ref[...])
    kr = kr_ref[...]
    ss_kr = jnp.sum(kr * kr, axis=-1, keepdims=True)
    kr_w = rope(kr * khn_ref[:, MLA_NOPE:])
    for h in range(MLA_HEADS):
        kn = kvf[:, h * MLA_NOPE:(h + 1) * MLA_NOPE]
        rs = lax.rsqrt((jnp.sum(kn * kn, axis=-1, keepdims=True) + ss_kr) * inv_qk + NORM_EPS)
        k_ref[:, h * MLA_QKP:h * MLA_QKP + MLA_NOPE] = (kn * rs * khn_ref[:, :MLA_NOPE]).astype(BF16)
        k_ref[:, h * MLA_QKP + MLA_NOPE:(h + 1) * MLA_QKP] = (kr_w * rs).astype(BF16)
    v_ref[...] = kvf[:, MLA_HEADS * MLA_NOPE:].astype(BF16)


def _mla_proj(cq, ckv, kr, rope_tabs, qn, wq, kvn, wkv, qhn, khn, with_q):
    t = cq.shape[0]
    tm = 256
    use_rope = rope_tabs is not None
    row = lambda i: (i, 0)
    const = lambda i: (0, 0)
    ins = [cq, ckv, kr]
    in_specs = [pl.BlockSpec((tm, Q_LORA), row), pl.BlockSpec((tm, KV_LORA), row), pl.BlockSpec((tm, LANES), row)]
    if use_rope:
        cos, sin = rope_tabs
        nrt = cos.shape[0] // tm
        ins += [cos, sin]
        in_specs += [pl.BlockSpec((tm, LANES), lambda i: (i % nrt, 0))] * 2
    ins += [qn, wq, kvn, wkv, qhn, khn]
    in_specs += [pl.BlockSpec((1, Q_LORA), const), pl.BlockSpec((Q_LORA, MLA_HEADS * MLA_QKP), const),
                 pl.BlockSpec((1, KV_LORA), const), pl.BlockSpec((KV_LORA, MLA_HEADS * (MLA_NOPE + MLA_V)), const),
                 pl.BlockSpec((1, MLA_QKP), const), pl.BlockSpec((1, MLA_QKP), const)]
    out_specs = [pl.BlockSpec((tm, MLA_HEADS * MLA_QKP), row), pl.BlockSpec((tm, MLA_W), row)]
    out_shape = [jax.ShapeDtypeStruct((t, MLA_HEADS * MLA_QKP), BF16), jax.ShapeDtypeStruct((t, MLA_W), BF16)]
    if with_q:
        out_specs = [pl.BlockSpec((tm, MLA_HEADS * MLA_QKP), row)] + out_specs
        out_shape = [jax.ShapeDtypeStruct((t, MLA_HEADS * MLA_QKP), BF16)] + out_shape
    return pl.pallas_call(
        functools.partial(_mla_kernel, use_rope=use_rope, with_q=with_q),
        grid=(t // tm,),
        in_specs=in_specs,
        out_specs=out_specs,
        out_shape=out_shape,
        compiler_params=_cparams(("parallel",), 40),
        name="mla_proj",
    )(*ins)


ATTN_CHAIN_ROWS = 128
ATTN_CHAINS = 4


def _attn_kernel(*refs, latent_keys, chains):
    if latent_keys:
        q_ref, kc_ref, vc_ref, kl_ref, vl_ref, o_ref = refs
    else:
        q_ref, kc_ref, vc_ref, o_ref = refs
    nt = (((1,), (1,)), ((), ()))
    sub = ATTN_CHAIN_ROWS

    def body(it, carry):
        for ch in range(chains):
            rows = pl.ds(pl.multiple_of((it * chains + ch) * sub, sub), sub)
            q = q_ref[rows, :]
            sc = lax.dot_general(q, kc_ref[...], nt, preferred_element_type=F32)
            m = jnp.max(sc, axis=-1, keepdims=True)
            if latent_keys:
                sl = lax.dot_general(q, kl_ref[...], nt, preferred_element_type=F32)
                m = jnp.maximum(m, jnp.max(sl, axis=-1, keepdims=True))
            pc = jnp.exp2(sc - m)
            den = jnp.sum(pc, axis=-1, keepdims=True)
            o = _bdot(pc.astype(BF16), vc_ref[...])
            if latent_keys:
                pl_ = jnp.exp2(sl - m)
                den = den + jnp.sum(pl_, axis=-1, keepdims=True)
                o = o + _bdot(pl_.astype(BF16), vl_ref[...])
            o_ref[rows, :] = (o / den).astype(BF16)
        return carry

    lax.fori_loop(0, q_ref.shape[0] // (chains * sub), body, 0)


def _attention(q, kc, vc, kl, vl, bsz):
    tq_total = q.shape[0]
    lq = tq_total // bsz
    lc = kc.shape[0] // bsz
    chains = min(ATTN_CHAINS, lq // ATTN_CHAIN_ROWS)
    latent_keys = kl is not None
    ins = [q, kc, vc]
    in_specs = [pl.BlockSpec((lq, MLA_QKP), lambda b, h: (b, h)),
                pl.BlockSpec((lc, MLA_QKP), lambda b, h: (b, h)),
                pl.BlockSpec((lc, MLA_V), lambda b, h: (b, h))]
    if latent_keys:
        ll = kl.shape[0] // bsz
        ins += [kl, vl]
        in_specs += [pl.BlockSpec((ll, MLA_QKP), lambda b, h: (b, h)),
                     pl.BlockSpec((ll, MLA_V), lambda b, h: (b, h))]
    return pl.pallas_call(
        functools.partial(_attn_kernel, latent_keys=latent_keys, chains=chains),
        grid=(bsz, MLA_HEADS),
        in_specs=in_specs,
        out_specs=pl.BlockSpec((lq, MLA_V), lambda b, h: (b, h)),
        out_shape=jax.ShapeDtypeStruct((tq_total, MLA_W), BF16),
        compiler_params=_cparams(("parallel", "parallel"), 48),
        name="attention",
    )(*ins)


def _out_proj_kernel(*refs, route):
    ssd_ref, pool_ref, att_ref, h_ref, mod_ref, nw_ref, w_ref = refs[:7]
    refs = refs[7:]
    if route:
        wr_hi_ref, wr_lo_ref, ho_ref, v_ref, r_ref = refs
    else:
        ho_ref, v_ref = refs
    mod = mod_ref[0]
    mix = (_bdot(ssd_ref[...], w_ref[0:SSD_W, :])
           + _bdot(pool_ref[...], w_ref[SSD_W:SSD_W + POOL_W, :])
           + _bdot(att_ref[...], w_ref[SSD_W + POOL_W:, :]))
    hn = h_ref[...] + mod[2:3, :] * mix
    ho_ref[...] = hn
    v = _rms(hn, nw_ref[...]) * (1.0 + mod[4:5, :]) + mod[3:4, :]
    vb = v.astype(BF16)
    v_ref[...] = vb
    if route:
        v_lo = (v - vb.astype(F32)).astype(BF16)
        logits = _bdot(vb, wr_hi_ref[...]) + _bdot(v_lo, wr_hi_ref[...]) + _bdot(vb, wr_lo_ref[...])
        lane = lax.broadcasted_iota(jnp.int32, logits.shape, 1).astype(F32)
        neg = -jnp.inf
        logits = jnp.where(lane < N_EXPERTS, logits, neg)
        m1 = jnp.max(logits, axis=-1, keepdims=True)
        i1 = jnp.min(jnp.where(logits == m1, lane, float(LANES)), axis=-1, keepdims=True)
        rest = jnp.where(lane == i1, neg, logits)
        m2 = jnp.max(rest, axis=-1, keepdims=True)
        i2 = jnp.min(jnp.where(rest == m2, lane, float(LANES)), axis=-1, keepdims=True)
        e2 = jnp.exp(m2 - m1)
        g1 = 1.0 / (1.0 + e2)
        g2 = e2 / (1.0 + e2)
        r_ref[...] = jnp.where(lane == 0, i1, jnp.where(lane == 1, i2, jnp.where(lane == 2, g1, jnp.where(lane == 3, g2, 0.0))))


def _out_proj(ssd, pool, att, h, mod, norm_w, w, rows_per_mod, router=None):
    t, d = h.shape
    tm = 256
    tiles_per_mod = rows_per_mod // tm
    row = lambda i: (i, 0)
    const = lambda i: (0, 0)
    route = router is not None
    ins = [ssd, pool, att, h, mod, norm_w, w]
    in_specs = [pl.BlockSpec((tm, SSD_W), row), pl.BlockSpec((tm, POOL_W), row), pl.BlockSpec((tm, MLA_W), row),
                pl.BlockSpec((tm, d), row), pl.BlockSpec((1, 6, d), lambda i: (i // tiles_per_mod, 0, 0)),
                pl.BlockSpec((1, d), const), pl.BlockSpec((d, d), const)]
    out_specs = [pl.BlockSpec((tm, d), row), pl.BlockSpec((tm, d), row)]
    out_shape = [jax.ShapeDtypeStruct((t, d), F32), jax.ShapeDtypeStruct((t, d), BF16)]
    if route:
        ins += list(router)
        in_specs += [pl.BlockSpec((d, LANES), const)] * 2
        out_specs.append(pl.BlockSpec((tm, LANES), row))
        out_shape.append(jax.ShapeDtypeStruct((t, LANES), F32))
    return pl.pallas_call(
        functools.partial(_out_proj_kernel, route=route),
        grid=(t // tm,),
        in_specs=in_specs,
        out_specs=out_specs,
        out_shape=out_shape,
        compiler_params=_cparams(("parallel",), 48),
        name="out_proj",
    )(*ins)


FFN_TM = 1024
FFN_TF = 256
FFN_SUB = 512


def _swiglu_rows(x_ref, wg, wu, wd, o_ref, rows):
    x = x_ref[rows, :]
    act = (_silu(_bdot(x, wg)) * _bdot(x, wu)).astype(BF16)
    o_ref[rows, :] += _bdot(act, wd)


def _ffn_kernel(v_ref, wg_ref, wu_ref, wd_ref, h_ref, mod_ref, o_ref, *, sub):
    f = pl.program_id(1)

    @pl.when(f == 0)
    def _():
        o_ref[...] = jnp.zeros_like(o_ref)

    wg = wg_ref[0].astype(BF16)
    wu = wu_ref[0].astype(BF16)
    wd = wd_ref[0].astype(BF16)
    for s in range(v_ref.shape[0] // sub):
        _swiglu_rows(v_ref, wg, wu, wd, o_ref, slice(s * sub, (s + 1) * sub))

    @pl.when(f == pl.num_programs(1) - 1)
    def _():
        o_ref[...] = h_ref[...] + mod_ref[0][5:6, :] * o_ref[...]


def _ffn(v, wg, wu, wd, j, h, mod, rows_per_mod):
    t, d = h.shape
    tm = min(FFN_TM, rows_per_mod)
    tf = FFN_TF
    tiles_per_mod = rows_per_mod // tm
    nf = wg.shape[2] // tf
    return pl.pallas_call(
        functools.partial(_ffn_kernel, sub=min(FFN_SUB, tm)),
        grid=(t // tm, nf),
        in_specs=[pl.BlockSpec((tm, d), lambda i, f: (i, 0)),
                  pl.BlockSpec((1, d, tf), lambda i, f: (j, 0, f)),
                  pl.BlockSpec((1, d, tf), lambda i, f: (j, 0, f)),
                  pl.BlockSpec((1, tf, d), lambda i, f: (j, f, 0)),
                  pl.BlockSpec((tm, d), lambda i, f: (i, 0), pipeline_mode=pl.Buffered(1)),
                  pl.BlockSpec((1, 6, d), lambda i, f: (i // tiles_per_mod, 0, 0))],
        out_specs=pl.BlockSpec((tm, d), lambda i, f: (i, 0)),
        out_shape=jax.ShapeDtypeStruct((t, d), F32),
        compiler_params=_cparams(("parallel", "arbitrary"), 56),
        name="ffn",
    )(v, wg, wu, wd, h, mod)


def _moe_kernel(te_ref, tn_ref, x_ref, wg_ref, wu_ref, wd_ref, o_ref, *, sub):
    i = pl.program_id(0)
    f = pl.program_id(1)
    n_rows = tn_ref[i]

    @pl.when(n_rows > 0)
    def _():
        @pl.when(f == 0)
        def _():
            o_ref[...] = jnp.zeros_like(o_ref)

        wg = wg_ref[0, 0].astype(BF16)
        wu = wu_ref[0, 0].astype(BF16)
        wd = wd_ref[0, 0].astype(BF16)
        _swiglu_rows(x_ref, wg, wu, wd, o_ref, slice(0, sub))
        for s in range(1, x_ref.shape[0] // sub):
            @pl.when(n_rows > s * sub)
            def _():
                _swiglu_rows(x_ref, wg, wu, wd, o_ref, slice(s * sub, (s + 1) * sub))


def _moe_experts(tile_expert, tile_rows, x_sorted, wg, wu, wd, j):
    r, d = x_sorted.shape
    tm, tf = FFN_TM, FFN_TF
    nf = wg.shape[3] // tf

    def fidx(i, f, tn):
        return jnp.where(tn[i] > 0, f, nf - 1)

    grid_spec = pltpu.PrefetchScalarGridSpec(
        num_scalar_prefetch=2,
        grid=(r // tm, nf),
        in_specs=[pl.BlockSpec((tm, d), lambda i, f, te, tn: (i, 0)),
                  pl.BlockSpec((1, 1, d, tf), lambda i, f, te, tn: (j, te[i], 0, fidx(i, f, tn))),
                  pl.BlockSpec((1, 1, d, tf), lambda i, f, te, tn: (j, te[i], 0, fidx(i, f, tn))),
                  pl.BlockSpec((1, 1, tf, d), lambda i, f, te, tn: (j, te[i], fidx(i, f, tn), 0))],
        out_specs=pl.BlockSpec((tm, d), lambda i, f, te, tn: (i, 0)),
    )
    return pl.pallas_call(
        functools.partial(_moe_kernel, sub=FFN_SUB),
        grid_spec=grid_spec,
        out_shape=jax.ShapeDtypeStruct((r, d), F32),
        compiler_params=_cparams(("arbitrary", "arbitrary"), 56),
        name="moe_experts",
    )(tile_expert, tile_rows, x_sorted, wg, wu, wd)


def _combine_kernel(h_ref, y0_ref, y1_ref, r_ref, mod_ref, o_ref):
    r = r_ref[...]
    mixed = r[:, 2:3] * y0_ref[...] + r[:, 3:4] * y1_ref[...]
    o_ref[...] = h_ref[...] + mod_ref[0][5:6, :] * mixed


def _combine(h, y0, y1, route, mod, rows_per_mod):
    t, d = h.shape
    tm = 512
    tiles_per_mod = rows_per_mod // tm
    row = lambda i: (i, 0)
    return pl.pallas_call(
        _combine_kernel,
        grid=(t // tm,),
        in_specs=[pl.BlockSpec((tm, d), row), pl.BlockSpec((tm, d), row), pl.BlockSpec((tm, d), row),
                  pl.BlockSpec((tm, LANES), row),
                  pl.BlockSpec((1, 6, d), lambda i: (i // tiles_per_mod, 0, 0))],
        out_specs=pl.BlockSpec((tm, d), row),
        out_shape=jax.ShapeDtypeStruct((t, d), F32),
        compiler_params=_cparams(("parallel",), 48),
        name="moe_combine",
    )(h, y0, y1, route, mod)


def _moe(v, route, h, mod, wg, wu, wd, j, rows_per_mod):
    t = v.shape[0]
    tm = FFN_TM
    n_assign = t * TOP_K
    rows = n_assign + N_EXPERTS * tm
    expert = route[:, :TOP_K].astype(jnp.int32).reshape(-1)
    onehot = (expert[:, None] == jnp.arange(N_EXPERTS, dtype=jnp.int32)[None, :]).astype(jnp.int32)
    csum = jnp.cumsum(onehot, axis=0)
    rank = jnp.sum(csum * onehot, axis=1) - 1
    counts = csum[-1]
    padded = ((counts + tm - 1) // tm) * tm
    ends = jnp.cumsum(padded)
    starts = ends - padded
    dest = starts[expert] + rank
    row_token = jnp.zeros((rows,), jnp.int32).at[dest].set(jnp.arange(n_assign, dtype=jnp.int32) // TOP_K)
    tile_start = jnp.arange(rows // tm, dtype=jnp.int32) * tm
    tile_expert = jnp.minimum(jnp.searchsorted(ends, tile_start, side="right"), N_EXPERTS - 1).astype(jnp.int32)
    tile_rows = jnp.clip((starts + counts)[tile_expert] - tile_start, 0, tm)
    tile_rows = jnp.where(tile_start < ends[-1], tile_rows, 0).astype(jnp.int32)
    last_valid = jnp.maximum(ends[-1] // tm - 1, 0)
    tile_expert = jnp.where(tile_rows > 0, tile_expert, tile_expert[last_valid])
    x_sorted = v.at[row_token].get(mode="promise_in_bounds")
    y_sorted = _moe_experts(tile_expert, tile_rows, x_sorted, wg, wu, wd, j)
    dest2 = dest.reshape(t, TOP_K)
    y0 = y_sorted.at[dest2[:, 0]].get(mode="promise_in_bounds")
    y1 = y_sorted.at[dest2[:, 1]].get(mode="promise_in_bounds")
    return _combine(h, y0, y1, route, mod, rows_per_mod)


def _pad_cols(a, n):
    return jnp.pad(a, ((0, 0), (0, n - a.shape[1])))


def _prep_w_in(w):
    z, xbc, dtr, pool, cq, ckv, kr = jnp.split(w, np.cumsum(IN_SIZES)[:-1].tolist(), axis=1)
    parts = [z, xbc, pool, cq, ckv, _pad_cols(kr, LANES), _pad_cols(dtr[:, :SSD_HEADS], LANES),
             _pad_cols(dtr[:, SSD_HEADS:], LANES)]
    return jnp.concatenate(parts, axis=1).astype(BF16)


def _prep_w_uq(w):
    w = w.reshape(Q_LORA, MLA_HEADS, MLA_QK)
    w = jnp.pad(w, ((0, 0), (0, 0), (0, MLA_QKP - MLA_QK)))
    return w.reshape(Q_LORA, MLA_HEADS * MLA_QKP).astype(BF16)


def _prep_w_ukv(w):
    w = w.reshape(KV_LORA, MLA_HEADS, MLA_NOPE + MLA_V)
    k = w[:, :, :MLA_NOPE].reshape(KV_LORA, MLA_HEADS * MLA_NOPE)
    v = w[:, :, MLA_NOPE:].reshape(KV_LORA, MLA_HEADS * MLA_V)
    return jnp.concatenate([k, v], axis=1).astype(BF16)


def _rope_tables(seq):
    quarter = MLA_ROPE // 4
    inv = ROPE_BASE ** (-jnp.arange(quarter, dtype=F32) / quarter)
    pos = jnp.arange(seq, dtype=jnp.int32)
    pos_row = (pos // GRID_W).astype(F32)
    pos_col = (pos % GRID_W).astype(F32)
    ang_row = pos_row[:, None] * inv[None, :]
    ang_col = pos_col[:, None] * inv[None, :]
    ang = jnp.concatenate([ang_row, ang_row, ang_col, ang_col], axis=1)
    sign = jnp.tile(jnp.concatenate([-jnp.ones((quarter,), F32), jnp.ones((quarter,), F32)]), 2)
    cos = jnp.concatenate([jnp.cos(ang), jnp.ones((seq, LANES - MLA_ROPE), F32)], axis=1)
    sin = jnp.concatenate([jnp.sin(ang) * sign[None, :], jnp.zeros((seq, LANES - MLA_ROPE), F32)], axis=1)
    return cos, sin


def _head_expand():
    k = np.arange(LANES)[:, None]
    j = np.arange(SSD_W)[None, :]
    return jnp.asarray((k == j // SSD_HEADDIM).astype(np.float32), dtype=BF16)


def _split_hi_lo(w):
    hi = w.astype(BF16)
    lo = (w - hi.astype(F32)).astype(BF16)
    return hi, lo


def kernel(x, c, ctx, c_ctx, ada_w, ada_b, norm_mix_w, norm_ffn_w, w_in, ssd_conv_w, ssd_conv_b, ssd_dt_bias, ssd_a_log, ssd_d, ssd_norm_w, pool_w, pool_scale, mla_q_norm_w, mla_w_uq, mla_kv_norm_w, mla_w_ukv, mla_q_head_norm_w, mla_k_head_norm_w, w_out, ffn_w_gate, ffn_w_up, ffn_w_down, moe_w_router, moe_w_gate, moe_w_up, moe_w_down):
    bsz, seq, d = x.shape
    lc = ctx.shape[1]
    depth = ada_w.shape[0]
    h_l = x.reshape(bsz * seq, d)
    h_c = ctx.reshape(bsz * lc, d)
    t_l, t_c = h_l.shape[0], h_c.shape[0]

    pad_rows = (-(bsz + 1)) % SUBLANES
    cvec = jnp.concatenate([c, c_ctx[None, :], jnp.zeros((pad_rows, d), F32)], axis=0)
    rope_tabs = _rope_tables(seq)
    expand = _head_expand()
    zero_state = jnp.zeros((bsz, 2, SSD_GROUPS, SSD_STATE, SSD_GW), F32)

    for i in range(depth):
        last = i == depth - 1
        j = i // 2
        mod = _ada(cvec, ada_w[i], ada_b[i][None, :])
        mod_l = mod[:bsz].reshape(bsz, 6, d)
        mod_c = mod[bsz:bsz + 1].reshape(1, 6, d)

        w_in_i = _prep_w_in(w_in[i])
        nmw = norm_mix_w[i][None, :]
        z_l, xbc_l, pool_l, cq_l, ckv_l, kr_l, dt_l = _in_proj(h_l, mod_l, nmw, w_in_i, seq)
        z_c, xbc_c, pool_c, cq_c, ckv_c, kr_c, dt_c = _in_proj(h_c, mod_c, nmw, w_in_i, t_c)

        conv_b = ssd_conv_b[i][None, :]
        dtb = _pad_cols(ssd_dt_bias[i], LANES)[:, None, :]
        alog = _pad_cols(ssd_a_log[i], LANES)[:, None, :]
        dsk = jnp.repeat(ssd_d[i], SSD_HEADDIM)[None, :]
        snw = ssd_norm_w[i][None, :]
        ssd_args = (ssd_conv_w[i], conv_b, dtb, alog, dsk, snw, expand, bsz)
        ssd_c, st_c = _ssd(xbc_c, z_c, dt_c, zero_state, *ssd_args)
        ssd_l, _ = _ssd(xbc_l, z_l, dt_l, st_c, *ssd_args)

        pw = pool_w[i].astype(BF16)
        psc = pool_scale[i][None, :]
        pool_out_l = _pool(pool_l, pw, psc, bsz)

        qhn = _pad_cols(mla_q_head_norm_w[i][None, :], MLA_QKP)
        khn = _pad_cols(mla_k_head_norm_w[i][None, :], MLA_QKP)
        mla_args = (mla_q_norm_w[i][None, :], _prep_w_uq(mla_w_uq[i]), mla_kv_norm_w[i][None, :],
                    _prep_w_ukv(mla_w_ukv[i]), qhn, khn)
        q_l, k_l, v_l = _mla_proj(cq_l, ckv_l, kr_l, rope_tabs, *mla_args, with_q=True)
        if last:
            k_c, v_c = _mla_proj(cq_c, ckv_c, kr_c, None, *mla_args, with_q=False)
        else:
            q_c, k_c, v_c = _mla_proj(cq_c, ckv_c, kr_c, None, *mla_args, with_q=True)
        att_l = _attention(q_l, k_c, v_c, k_l, v_l, bsz)

        w_out_i = w_out[i].astype(BF16)
        nfw = norm_ffn_w[i][None, :]
        dense = i % 2 == 0
        router = None if dense else _split_hi_lo(_pad_cols(moe_w_router[j], LANES))
        res_l = _out_proj(ssd_l, pool_out_l, att_l, h_l, mod_l, nfw, w_out_i, seq, router)
        if dense:
            wg, wu, wd = ffn_w_gate, ffn_w_up, ffn_w_down
            h_l = _ffn(res_l[1], wg, wu, wd, j, res_l[0], mod_l, seq)
        else:
            wg, wu, wd = moe_w_gate, moe_w_up, moe_w_down
            h_l = _moe(res_l[1], res_l[2], res_l[0], mod_l, wg, wu, wd, j, seq)
        if not last:
            pool_out_c = _pool(pool_c, pw, psc, bsz)
            att_c = _attention(q_c, k_c, v_c, None, None, bsz)
            res_c = _out_proj(ssd_c, pool_out_c, att_c, h_c, mod_c, nfw, w_out_i, t_c, router)
            if dense:
                h_c = _ffn(res_c[1], wg, wu, wd, j, res_c[0], mod_c, t_c)
            else:
                h_c = _moe(res_c[1], res_c[2], res_c[0], mod_c, wg, wu, wd, j, t_c)
    return h_l.reshape(bsz, seq, d)
```

```python
import functools

import numpy as np
import jax
import jax.numpy as jnp
from jax import lax
from jax.experimental import pallas as pl
from jax.experimental.pallas import tpu as pltpu

F32 = jnp.float32
BF16 = jnp.bfloat16

D_MODEL = 2048
GRID_W = 64
SSD_HEADS = 12
SSD_HEADDIM = 64
SSD_W = SSD_HEADS * SSD_HEADDIM
SSD_GROUPS = 2
SSD_HPG = SSD_HEADS // SSD_GROUPS
SSD_STATE = 128
SSD_CONV = 4
SSD_CHUNK = 128
SSD_BC = SSD_GROUPS * SSD_STATE
SSD_CONV_CH = SSD_W + 2 * SSD_BC
SSD_GW = SSD_HPG * SSD_HEADDIM
POOL_WINDOWS = (2, 4, 8, 16)
POOL_GROUPS = 4
POOL_GC = 128
POOL_W = POOL_GROUPS * POOL_GC
POOL_HALO = 8
MLA_HEADS = 6
MLA_NOPE = 128
MLA_ROPE = 64
MLA_QK = MLA_NOPE + MLA_ROPE
MLA_QKP = 256
MLA_V = 128
MLA_W = MLA_HEADS * MLA_V
Q_LORA = 512
KV_LORA = 512
ROPE_BASE = 10000.0
LOG2_E = 1.4426950408889634
FFN_DIM = 5632
N_EXPERTS = 8
TOP_K = 2
NORM_EPS = 1e-6
IN_SIZES = (SSD_W, SSD_CONV_CH, 2 * SSD_HEADS, POOL_W, Q_LORA, KV_LORA, MLA_ROPE)
LANES = 128
SUBLANES = 8
IN_Z = 0
IN_XBC = IN_Z + SSD_W
IN_POOL = IN_XBC + SSD_CONV_CH
IN_CQ = IN_POOL + POOL_W
IN_CKV = IN_CQ + Q_LORA
IN_KR = IN_CKV + KV_LORA
IN_DT0 = IN_KR + LANES
IN_DT1 = IN_DT0 + LANES
IN_COLS = IN_DT1 + LANES
MIB = 1024 * 1024


def _cparams(sem, vmem_mib):
    return pltpu.CompilerParams(dimension_semantics=sem, vmem_limit_bytes=vmem_mib * MIB)


def _sigmoid(x):
    return 1.0 / (1.0 + jnp.exp(-x))


def _silu(x):
    return x * _sigmoid(x)


def _softplus(x):
    return jnp.maximum(x, 0.0) + jnp.log1p(jnp.exp(-jnp.abs(x)))


def _rms(x, w):
    return x * lax.rsqrt(jnp.mean(x * x, axis=-1, keepdims=True) + NORM_EPS) * w


def _bdot(a, b):
    return jnp.dot(a, b, preferred_element_type=F32)


def _split_bf16(x, passes):
    parts = []
    r = x
    for _ in range(passes):
        p = r.astype(BF16)
        parts.append(p)
        r = r - p.astype(F32)
    return parts


def _ldot(m_bf16, x, passes):
    out = None
    for p in _split_bf16(x, passes):
        t = _bdot(m_bf16, p)
        out = t if out is None else out + t
    return out


def _rdot(x, m_bf16, passes):
    out = None
    for p in _split_bf16(x, passes):
        t = _bdot(p, m_bf16)
        out = t if out is None else out + t
    return out


def _ada_kernel(c_ref, w_ref, b_ref, o_ref):
    s = _silu(c_ref[...]).astype(BF16)
    o_ref[...] = _bdot(s, w_ref[...].astype(BF16)) + b_ref[...]


def _ada(cvec, w, b):
    rows, d = cvec.shape
    n = w.shape[1]
    tn = 1024
    return pl.pallas_call(
        _ada_kernel,
        grid=(n // tn,),
        in_specs=[pl.BlockSpec((rows, d), lambda j: (0, 0)),
                  pl.BlockSpec((d, tn), lambda j: (0, j)),
                  pl.BlockSpec((1, tn), lambda j: (0, j))],
        out_specs=pl.BlockSpec((rows, tn), lambda j: (0, j)),
        out_shape=jax.ShapeDtypeStruct((rows, n), F32),
        compiler_params=_cparams(("arbitrary",), 40),
        name="ada",
    )(cvec, w, b)


def _in_proj_kernel(h_ref, mod_ref, nw_ref, w_ref, z_ref, xbc_ref, pool_ref, cq_ref, ckv_ref,
                    kr_ref, dt_ref):
    mod = mod_ref[0]
    u = _rms(h_ref[...], nw_ref[...]) * (1.0 + mod[1:2, :]) + mod[0:1, :]
    ub = u.astype(BF16)
    z_ref[...] = _bdot(ub, w_ref[:, IN_Z:IN_XBC])
    xbc_ref[...] = _bdot(ub, w_ref[:, IN_XBC:IN_POOL])
    pool_ref[...] = _bdot(ub, w_ref[:, IN_POOL:IN_CQ])
    cq_ref[...] = _bdot(ub, w_ref[:, IN_CQ:IN_CKV])
    ckv_ref[...] = _bdot(ub, w_ref[:, IN_CKV:IN_KR])
    kr_ref[...] = _bdot(ub, w_ref[:, IN_KR:IN_DT0])
    dt_ref[0] = _bdot(ub, w_ref[:, IN_DT0:IN_DT1])
    dt_ref[1] = _bdot(ub, w_ref[:, IN_DT1:IN_COLS])


def _in_proj(h, mod, norm_w, w, rows_per_mod):
    t, d = h.shape
    tm = 256
    tiles_per_mod = rows_per_mod // tm
    widths = (SSD_W, SSD_CONV_CH, POOL_W, Q_LORA, KV_LORA, LANES)
    row = lambda i: (i, 0)
    return pl.pallas_call(
        _in_proj_kernel,
        grid=(t // tm,),
        in_specs=[pl.BlockSpec((tm, d), row),
                  pl.BlockSpec((1, 6, d), lambda i: (i // tiles_per_mod, 0, 0)),
                  pl.BlockSpec((1, d), lambda i: (0, 0)),
                  pl.BlockSpec((d, IN_COLS), lambda i: (0, 0))],
        out_specs=[pl.BlockSpec((tm, wd), row) for wd in widths]
        + [pl.BlockSpec((2, tm, LANES), lambda i: (0, i, 0))],
        out_shape=[jax.ShapeDtypeStruct((t, wd), F32) for wd in widths]
        + [jax.ShapeDtypeStruct((2, t, LANES), F32)],
        compiler_params=_cparams(("parallel",), 56),
        name="in_proj",
    )(h, mod, norm_w, w)


def _ssd_kernel(xm_ref, xp_ref, xn_ref, z_ref, dt_ref, init_ref, cw_ref, cb_ref, dtb_ref, alog_ref,
                dsk_ref, nw_ref, e_ref, y_ref, fin_ref, yf_scr, st_scr, xe_scr, xbc_scr, *, nc):
    q = SSD_CHUNK
    t = pl.program_id(1)
    d = t // nc
    cpos = t % nc
    c = jnp.where(d == 0, cpos, nc - 1 - cpos)
    r0 = pl.multiple_of(c * q, q)

    @pl.when(cpos == 0)
    def _():
        st_scr[...] = init_ref[0, 0]

    @pl.when(d == 0)
    def _():
        xe_scr[0:SUBLANES, :] = jnp.where(c == 0, 0.0, xp_ref[...])
        xe_scr[SUBLANES:SUBLANES + q, :] = xm_ref[...]
        xe_scr[SUBLANES + q:, :] = jnp.where(c == nc - 1, 0.0, xn_ref[...])
        cw = cw_ref[...]
        off = SUBLANES - SSD_CONV // 2
        acc = cb_ref[...]
        for k in range(SSD_CONV):
            acc = acc + cw[k:k + 1, :] * xe_scr[off + k:off + k + q, :]
        xbc_scr[pl.ds(r0, q), :] = _silu(acc)

    xbc = xbc_scr[pl.ds(r0, q), :]
    xs = xbc[:, :SSD_W]

    lane = lax.broadcasted_iota(jnp.int32, (q, LANES), 1)
    row = lax.broadcasted_iota(jnp.int32, (q, LANES), 0)
    dt = jnp.where(lane < SSD_HEADS, _softplus(dt_ref[0] + dtb_ref[0]), 0.0)
    adt = dt * (-jnp.exp(alog_ref[0]))
    causal = (row - lane) * (1 - 2 * d) >= 0
    tri = jnp.where(causal, 1.0, 0.0).astype(BF16)
    cs = _ldot(tri, adt, 3)
    tot = jnp.sum(adt, axis=0, keepdims=True)
    e = e_ref[...]
    dtx = _rdot(dt, e, 2)
    ecsx = _rdot(jnp.exp(cs), e, 2)
    wx = _rdot(dt * jnp.exp(tot - cs), e, 2)
    etot = _rdot(jnp.broadcast_to(jnp.exp(tot), (SUBLANES, LANES)), e, 2)[0:1, :]
    xdt = xs * dtx
    xw = (xs * wx).astype(BF16)
    cs_t = cs.T
    low_half = lane < SSD_HEADDIM

    ys = []
    for g in range(SSD_GROUPS):
        bg = xbc[:, SSD_W + g * SSD_STATE:SSD_W + (g + 1) * SSD_STATE]
        cg = xbc[:, SSD_W + SSD_BC + g * SSD_STATE:SSD_W + SSD_BC + (g + 1) * SSD_STATE]
        cgb = cg.astype(BF16)
        cb = lax.dot_general(cgb, bg.astype(BF16), (((1,), (1,)), ((), ())),
                             preferred_element_type=F32)
        gs = slice(g * SSD_GW, (g + 1) * SSD_GW)
        st = st_scr[g]
        y_off = _bdot(cgb, st.astype(BF16)) * ecsx[:, gs]
        st_scr[g] = st * etot[:, gs] + _bdot(bg.T.astype(BF16), xw[:, gs])
        for pr in range(SSD_HPG // 2):
            xpair = xdt[:, g * SSD_GW + pr * LANES:g * SSD_GW + (pr + 1) * LANES]
            y_pair = y_off[:, pr * LANES:(pr + 1) * LANES]
            for k in range(2):
                h = g * SSD_HPG + pr * 2 + k
                dcs = cs[:, h:h + 1] - cs_t[h:h + 1, :]
                m = (cb * jnp.exp(jnp.where(causal, dcs, -1e30))).astype(BF16)
                xh = jnp.where(low_half if k == 0 else jnp.logical_not(low_half), xpair, 0.0)
                y_pair = y_pair + _bdot(m, xh.astype(BF16))
            ys.append(y_pair)
    y = jnp.concatenate(ys, axis=1)

    @pl.when(d == 0)
    def _():
        yf_scr[pl.ds(r0, q), :] = y

    @pl.when(d == 1)
    def _():
        g_ = (yf_scr[pl.ds(r0, q), :] + y + xs * dsk_ref[...]) * _silu(z_ref[...])
        y_ref[...] = _rms(g_, nw_ref[...]).astype(BF16)

    @pl.when(cpos == nc - 1)
    def _():
        fin_ref[0, 0] = st_scr[...]


def _ssd(xbc, z, dt, init, conv_w, conv_b, dt_bias, a_log, d_skip, norm_w, expand, bsz):
    t = xbc.shape[0]
    q = SSD_CHUNK
    nc = t // bsz // q
    tiles8 = q // SUBLANES

    def chunk(tt):
        cp = tt % nc
        return jnp.where(tt < nc, cp, nc - 1 - cp)

    def out_chunk(tt):
        return jnp.where(tt < nc, nc - 1, 2 * nc - 1 - tt)

    def conv_chunk(tt):
        return jnp.minimum(tt, nc - 1)

    const2 = lambda b, tt: (0, 0)
    state_spec = pl.BlockSpec((1, 1, SSD_GROUPS, SSD_STATE, SSD_GW), lambda b, tt: (b, tt // nc, 0, 0, 0))
    state_shape = jax.ShapeDtypeStruct((bsz, 2, SSD_GROUPS, SSD_STATE, SSD_GW), F32)
    return pl.pallas_call(
        functools.partial(_ssd_kernel, nc=nc),
        grid=(bsz, 2 * nc),
        in_specs=[
            pl.BlockSpec((q, SSD_CONV_CH), lambda b, tt: (b * nc + conv_chunk(tt), 0)),
            pl.BlockSpec((SUBLANES, SSD_CONV_CH),
                         lambda b, tt: (jnp.maximum((b * nc + conv_chunk(tt)) * tiles8 - 1, 0), 0)),
            pl.BlockSpec((SUBLANES, SSD_CONV_CH),
                         lambda b, tt: (jnp.minimum((b * nc + conv_chunk(tt) + 1) * tiles8, t // SUBLANES - 1), 0)),
            pl.BlockSpec((q, SSD_W), lambda b, tt: (b * nc + out_chunk(tt), 0)),
            pl.BlockSpec((1, q, LANES), lambda b, tt: (tt // nc, b * nc + chunk(tt), 0)),
            state_spec,
            pl.BlockSpec((SSD_CONV, SSD_CONV_CH), const2),
            pl.BlockSpec((1, SSD_CONV_CH), const2),
            pl.BlockSpec((1, 1, LANES), lambda b, tt: (tt // nc, 0, 0)),
            pl.BlockSpec((1, 1, LANES), lambda b, tt: (tt // nc, 0, 0)),
            pl.BlockSpec((1, SSD_W), const2),
            pl.BlockSpec((1, SSD_W), const2),
            pl.BlockSpec((LANES, SSD_W), const2),
        ],
        out_specs=[pl.BlockSpec((q, SSD_W), lambda b, tt: (b * nc + out_chunk(tt), 0)), state_spec],
        out_shape=[jax.ShapeDtypeStruct((t, SSD_W), BF16), state_shape],
        scratch_shapes=[pltpu.VMEM((nc * q, SSD_W), F32),
                        pltpu.VMEM((SSD_GROUPS, SSD_STATE, SSD_GW), F32),
                        pltpu.VMEM((q + 2 * SUBLANES, SSD_CONV_CH), F32),
                        pltpu.VMEM((nc * q, SSD_CONV_CH), F32)],
        compiler_params=_cparams(("parallel", "arbitrary"), 48),
        name="ssd",
    )(xbc, xbc, xbc, z, dt, init, conv_w, conv_b, dt_bias, a_log, d_skip, norm_w, expand)


def _pool_kernel(um_ref, up_ref, un_ref, pw_ref, ps_ref, o_ref, pe_scr, *, seq):
    tp = um_ref.shape[0]
    j = pl.program_id(1)
    pe_scr[0:POOL_HALO, :] = jnp.where(j == 0, 0.0, up_ref[...])
    pe_scr[POOL_HALO:POOL_HALO + tp, :] = um_ref[...]
    pe_scr[POOL_HALO + tp:, :] = jnp.where(j == pl.num_programs(1) - 1, 0.0, un_ref[...])
    pos = j * tp + lax.broadcasted_iota(jnp.int32, (tp, 1), 0)
    for g, w in enumerate(POOL_WINDOWS):
        cols = slice(g * POOL_GC, (g + 1) * POOL_GC)
        start = POOL_HALO - w // 2
        win = pe_scr[start:start + tp, cols]
        for k in range(1, w):
            win = win + pe_scr[start + k:start + k + tp, cols]
        lo = jnp.maximum(pos - w // 2, 0)
        hi = jnp.minimum(pos - w // 2 + w, seq)
        pooled = win / (hi - lo).astype(F32) - um_ref[:, cols]
        o_ref[:, cols] = (_bdot(pooled.astype(BF16), pw_ref[g]) * ps_ref[:, cols]).astype(BF16)


def _pool(u, pool_w, pool_scale, bsz):
    t = u.shape[0]
    seq = t // bsz
    tp = 256
    nt = seq // tp
    tiles8 = tp // SUBLANES
    return pl.pallas_call(
        functools.partial(_pool_kernel, seq=seq),
        grid=(bsz, nt),
        in_specs=[
            pl.BlockSpec((tp, POOL_W), lambda b, j: (b * nt + j, 0)),
            pl.BlockSpec((SUBLANES, POOL_W), lambda b, j: (jnp.maximum((b * nt + j) * tiles8 - 1, 0), 0)),
            pl.BlockSpec((SUBLANES, POOL_W),
                         lambda b, j: (jnp.minimum((b * nt + j + 1) * tiles8, t // SUBLANES - 1), 0)),
            pl.BlockSpec((POOL_GROUPS, POOL_GC, POOL_GC), lambda b, j: (0, 0, 0)),
            pl.BlockSpec((1, POOL_W), lambda b, j: (0, 0)),
        ],
        out_specs=pl.BlockSpec((tp, POOL_W), lambda b, j: (b * nt + j, 0)),
        out_shape=jax.ShapeDtypeStruct((t, POOL_W), BF16),
        scratch_shapes=[pltpu.VMEM((tp + 2 * POOL_HALO, POOL_W), F32)],
        compiler_params=_cparams(("parallel", "parallel"), 32),
        name="pool",
    )(u, u, u, pool_w, pool_scale)


def _rope(r, cos, sin_signed):
    lane = lax.broadcasted_iota(jnp.int32, r.shape, 1)
    quarter = MLA_ROPE // 4
    swapped = jnp.where(lane % (2 * quarter) < quarter,
                        pltpu.roll(r, LANES - quarter, 1), pltpu.roll(r, quarter, 1))
    return r * cos + swapped * sin_signed


def _mla_kernel(*refs, use_rope, with_q):
    if use_rope:
        cq_ref, ckv_ref, kr_ref, cos_ref, sin_ref = refs[:5]
        refs = refs[5:]
    else:
        cq_ref, ckv_ref, kr_ref = refs[:3]
        refs = refs[3:]
    qn_ref, wq_ref, kvn_ref, wkv_ref, qhn_ref, khn_ref = refs[:6]
    outs = refs[6:]
    if with_q:
        q_ref, k_ref, v_ref = outs
    else:
        k_ref, v_ref = outs
    rope = (lambda r: _rope(r, cos_ref[...], sin_ref[...])) if use_rope else (lambda r: r)
    inv_qk = 1.0 / MLA_QK

    if with_q:
        qf = _bdot(_rms(cq_ref[...], qn_ref[...]).astype(BF16), wq_ref[...])
        scale = MLA_QK ** -0.5 * LOG2_E
        for h in range(MLA_HEADS):
            qh = qf[:, h * MLA_QKP:(h + 1) * MLA_QKP]
            rs = lax.rsqrt(jnp.sum(qh * qh, axis=-1, keepdims=True) * inv_qk + NORM_EPS) * scale
            qh = qh * rs * qhn_ref[...]
            q_ref[:, h * MLA_QKP:h * MLA_QKP + MLA_NOPE] = qh[:, :MLA_NOPE].astype(BF16)
            q_ref[:, h * MLA_QKP + MLA_NOPE:(h + 1) * MLA_QKP] = rope(qh[:, MLA_NOPE:]).astype(BF16)

    kvf = _bdot(_rms(ckv_ref[...], kvn_ref[...]).astype(BF16), wkv_ref[...])
    kr = kr_ref[...]
    ss_kr = jnp.sum(kr * kr, axis=-1, keepdims=True)
    kr_w = rope(kr * khn_ref[:, MLA_NOPE:])
    for h in range(MLA_HEADS):
        kn = kvf[:, h * MLA_NOPE:(h + 1) * MLA_NOPE]
        rs = lax.rsqrt((jnp.sum(kn * kn, axis=-1, keepdims=True) + ss_kr) * inv_qk + NORM_EPS)
        k_ref[h * MLA_QKP:h * MLA_QKP + MLA_NOPE, :] = (kn * rs * khn_ref[:, :MLA_NOPE]).T.astype(BF16)
        k_ref[h * MLA_QKP + MLA_NOPE:(h + 1) * MLA_QKP, :] = (kr_w * rs).T.astype(BF16)
    v_ref[...] = kvf[:, MLA_HEADS * MLA_NOPE:].astype(BF16)


def _mla_proj(cq, ckv, kr, rope_tabs, qn, wq, kvn, wkv, qhn, khn, with_q):
    t = cq.shape[0]
    tm = 256
    use_rope = rope_tabs is not None
    row = lambda i: (i, 0)
    const = lambda i: (0, 0)
    ins = [cq, ckv, kr]
    in_specs = [pl.BlockSpec((tm, Q_LORA), row), pl.BlockSpec((tm, KV_LORA), row), pl.BlockSpec((tm, LANES), row)]
    if use_rope:
        cos, sin = rope_tabs
        nrt = cos.shape[0] // tm
        ins += [cos, sin]
        in_specs += [pl.BlockSpec((tm, LANES), lambda i: (i % nrt, 0))] * 2
    ins += [qn, wq, kvn, wkv, qhn, khn]
    in_specs += [pl.BlockSpec((1, Q_LORA), const), pl.BlockSpec((Q_LORA, MLA_HEADS * MLA_QKP), const),
                 pl.BlockSpec((1, KV_LORA), const), pl.BlockSpec((KV_LORA, MLA_HEADS * (MLA_NOPE + MLA_V)), const),
                 pl.BlockSpec((1, MLA_QKP), const), pl.BlockSpec((1, MLA_QKP), const)]
    out_specs = [pl.BlockSpec((MLA_HEADS * MLA_QKP, tm), lambda i: (0, i)), pl.BlockSpec((tm, MLA_W), row)]
    out_shape = [jax.ShapeDtypeStruct((MLA_HEADS * MLA_QKP, t), BF16), jax.ShapeDtypeStruct((t, MLA_W), BF16)]
    if with_q:
        out_specs = [pl.BlockSpec((tm, MLA_HEADS * MLA_QKP), row)] + out_specs
        out_shape = [jax.ShapeDtypeStruct((t, MLA_HEADS * MLA_QKP), BF16)] + out_shape
    return pl.pallas_call(
        functools.partial(_mla_kernel, use_rope=use_rope, with_q=with_q),
        grid=(t // tm,),
        in_specs=in_specs,
        out_specs=out_specs,
        out_shape=out_shape,
        compiler_params=_cparams(("parallel",), 40),
        name="mla_proj",
    )(*ins)


ATTN_CHAIN_ROWS = 256
ATTN_CHAINS = 2


def _attn_kernel(*refs, latent_keys, chains):
    if latent_keys:
        q_ref, kc_ref, vc_ref, kl_ref, vl_ref, o_ref = refs
    else:
        q_ref, kc_ref, vc_ref, o_ref = refs
    sub = ATTN_CHAIN_ROWS

    def body(it, carry):
        for ch in range(chains):
            rows = pl.ds(pl.multiple_of((it * chains + ch) * sub, sub), sub)
            q = q_ref[rows, :]
            sc = _bdot(q, kc_ref[...])
            m = jnp.max(sc, axis=-1, keepdims=True)
            if latent_keys:
                sl = _bdot(q, kl_ref[...])
                m = jnp.maximum(m, jnp.max(sl, axis=-1, keepdims=True))
            pc = jnp.exp2(sc - m)
            den = jnp.sum(pc, axis=-1, keepdims=True)
            o = _bdot(pc.astype(BF16), vc_ref[...])
            if latent_keys:
                pl_ = jnp.exp2(sl - m)
                den = den + jnp.sum(pl_, axis=-1, keepdims=True)
                o = o + _bdot(pl_.astype(BF16), vl_ref[...])
            o_ref[rows, :] = (o / den).astype(BF16)
        return carry

    lax.fori_loop(0, q_ref.shape[0] // (chains * sub), body, 0)


def _attention(q, kc, vc, kl, vl, bsz):
    tq_total = q.shape[0]
    lq = tq_total // bsz
    lc = kc.shape[1] // bsz
    chains = min(ATTN_CHAINS, lq // ATTN_CHAIN_ROWS)
    latent_keys = kl is not None
    ins = [q, kc, vc]
    in_specs = [pl.BlockSpec((lq, MLA_QKP), lambda b, h: (b, h)),
                pl.BlockSpec((MLA_QKP, lc), lambda b, h: (h, b)),
                pl.BlockSpec((lc, MLA_V), lambda b, h: (b, h))]
    if latent_keys:
        ll = kl.shape[1] // bsz
        ins += [kl, vl]
        in_specs += [pl.BlockSpec((MLA_QKP, ll), lambda b, h: (h, b)),
                     pl.BlockSpec((ll, MLA_V), lambda b, h: (b, h))]
    return pl.pallas_call(
        functools.partial(_attn_kernel, latent_keys=latent_keys, chains=chains),
        grid=(bsz, MLA_HEADS),
        in_specs=in_specs,
        out_specs=pl.BlockSpec((lq, MLA_V), lambda b, h: (b, h)),
        out_shape=jax.ShapeDtypeStruct((tq_total, MLA_W), BF16),
        compiler_params=_cparams(("parallel", "parallel"), 48),
        name="attention",
    )(*ins)


def _out_proj_kernel(*refs, route):
    ssd_ref, pool_ref, att_ref, h_ref, mod_ref, nw_ref, w_ref = refs[:7]
    refs = refs[7:]
    if route:
        wr_hi_ref, wr_lo_ref, ho_ref, v_ref, r_ref = refs
    else:
        ho_ref, v_ref = refs
    mod = mod_ref[0]
    mix = (_bdot(ssd_ref[...], w_ref[0:SSD_W, :])
           + _bdot(pool_ref[...], w_ref[SSD_W:SSD_W + POOL_W, :])
           + _bdot(att_ref[...], w_ref[SSD_W + POOL_W:, :]))
    hn = h_ref[...] + mod[2:3, :] * mix
    ho_ref[...] = hn
    v = _rms(hn, nw_ref[...]) * (1.0 + mod[4:5, :]) + mod[3:4, :]
    vb = v.astype(BF16)
    v_ref[...] = vb
    if route:
        v_lo = (v - vb.astype(F32)).astype(BF16)
        logits = _bdot(vb, wr_hi_ref[...]) + _bdot(v_lo, wr_hi_ref[...]) + _bdot(vb, wr_lo_ref[...])
        lane = lax.broadcasted_iota(jnp.int32, logits.shape, 1).astype(F32)
        neg = -jnp.inf
        logits = jnp.where(lane < N_EXPERTS, logits, neg)
        m1 = jnp.max(logits, axis=-1, keepdims=True)
        i1 = jnp.min(jnp.where(logits == m1, lane, float(LANES)), axis=-1, keepdims=True)
        rest = jnp.where(lane == i1, neg, logits)
        m2 = jnp.max(rest, axis=-1, keepdims=True)
        i2 = jnp.min(jnp.where(rest == m2, lane, float(LANES)), axis=-1, keepdims=True)
        e2 = jnp.exp(m2 - m1)
        g1 = 1.0 / (1.0 + e2)
        g2 = e2 / (1.0 + e2)
        r_ref[...] = jnp.where(lane == 0, i1, jnp.where(lane == 1, i2, jnp.where(lane == 2, g1, jnp.where(lane == 3, g2, 0.0))))


def _out_proj(ssd, pool, att, h, mod, norm_w, w, rows_per_mod, router=None):
    t, d = h.shape
    tm = 256
    tiles_per_mod = rows_per_mod // tm
    row = lambda i: (i, 0)
    const = lambda i: (0, 0)
    route = router is not None
    ins = [ssd, pool, att, h, mod, norm_w, w]
    in_specs = [pl.BlockSpec((tm, SSD_W), row), pl.BlockSpec((tm, POOL_W), row), pl.BlockSpec((tm, MLA_W), row),
                pl.BlockSpec((tm, d), row), pl.BlockSpec((1, 6, d), lambda i: (i // tiles_per_mod, 0, 0)),
                pl.BlockSpec((1, d), const), pl.BlockSpec((d, d), const)]
    out_specs = [pl.BlockSpec((tm, d), row), pl.BlockSpec((tm, d), row)]
    out_shape = [jax.ShapeDtypeStruct((t, d), F32), jax.ShapeDtypeStruct((t, d), BF16)]
    if route:
        ins += list(router)
        in_specs += [pl.BlockSpec((d, LANES), const)] * 2
        out_specs.append(pl.BlockSpec((tm, LANES), row))
        out_shape.append(jax.ShapeDtypeStruct((t, LANES), F32))
    return pl.pallas_call(
        functools.partial(_out_proj_kernel, route=route),
        grid=(t // tm,),
        in_specs=in_specs,
        out_specs=out_specs,
        out_shape=out_shape,
        compiler_params=_cparams(("parallel",), 48),
        name="out_proj",
    )(*ins)


FFN_TM = 1024
FFN_TF = 256
FFN_SUB = 512


def _swiglu_rows(x_ref, wg, wu, wd, o_ref, rows):
    x = x_ref[rows, :]
    act = (_silu(_bdot(x, wg)) * _bdot(x, wu)).astype(BF16)
    o_ref[rows, :] += _bdot(act, wd)


def _ffn_kernel(v_ref, wg_ref, wu_ref, wd_ref, h_ref, mod_ref, o_ref, *, sub):
    f = pl.program_id(1)

    @pl.when(f == 0)
    def _():
        o_ref[...] = jnp.zeros_like(o_ref)

    wg = wg_ref[0].astype(BF16)
    wu = wu_ref[0].astype(BF16)
    wd = wd_ref[0].astype(BF16)
    for s in range(v_ref.shape[0] // sub):
        _swiglu_rows(v_ref, wg, wu, wd, o_ref, slice(s * sub, (s + 1) * sub))

    @pl.when(f == pl.num_programs(1) - 1)
    def _():
        o_ref[...] = h_ref[...] + mod_ref[0][5:6, :] * o_ref[...]


def _ffn(v, wg, wu, wd, j, h, mod, rows_per_mod):
    t, d = h.shape
    tm = min(FFN_TM, rows_per_mod)
    tf = FFN_TF
    tiles_per_mod = rows_per_mod // tm
    nf = wg.shape[2] // tf
    return pl.pallas_call(
        functools.partial(_ffn_kernel, sub=min(FFN_SUB, tm)),
        grid=(t // tm, nf),
        in_specs=[pl.BlockSpec((tm, d), lambda i, f: (i, 0)),
                  pl.BlockSpec((1, d, tf), lambda i, f: (j, 0, f)),
                  pl.BlockSpec((1, d, tf), lambda i, f: (j, 0, f)),
                  pl.BlockSpec((1, tf, d), lambda i, f: (j, f, 0)),
                  pl.BlockSpec((tm, d), lambda i, f: (i, 0), pipeline_mode=pl.Buffered(1)),
                  pl.BlockSpec((1, 6, d), lambda i, f: (i // tiles_per_mod, 0, 0))],
        out_specs=pl.BlockSpec((tm, d), lambda i, f: (i, 0)),
        out_shape=jax.ShapeDtypeStruct((t, d), F32),
        compiler_params=_cparams(("parallel", "arbitrary"), 56),
        name="ffn",
    )(v, wg, wu, wd, h, mod)


def _moe_kernel(te_ref, tn_ref, x_ref, wg_ref, wu_ref, wd_ref, o_ref, *, sub):
    i = pl.program_id(0)
    f = pl.program_id(1)
    n_rows = tn_ref[i]

    @pl.when(n_rows > 0)
    def _():
        @pl.when(f == 0)
        def _():
            o_ref[...] = jnp.zeros_like(o_ref)

        wg = wg_ref[0, 0].astype(BF16)
        wu = wu_ref[0, 0].astype(BF16)
        wd = wd_ref[0, 0].astype(BF16)
        _swiglu_rows(x_ref, wg, wu, wd, o_ref, slice(0, sub))
        for s in range(1, x_ref.shape[0] // sub):
            @pl.when(n_rows > s * sub)
            def _():
                _swiglu_rows(x_ref, wg, wu, wd, o_ref, slice(s * sub, (s + 1) * sub))


def _moe_experts(tile_expert, tile_rows, x_sorted, wg, wu, wd, j):
    r, d = x_sorted.shape
    tm, tf = FFN_TM, FFN_TF
    nf = wg.shape[3] // tf

    def fidx(i, f, tn):
        return jnp.where(tn[i] > 0, f, nf - 1)

    grid_spec = pltpu.PrefetchScalarGridSpec(
        num_scalar_prefetch=2,
        grid=(r // tm, nf),
        in_specs=[pl.BlockSpec((tm, d), lambda i, f, te, tn: (i, 0)),
                  pl.BlockSpec((1, 1, d, tf), lambda i, f, te, tn: (j, te[i], 0, fidx(i, f, tn))),
                  pl.BlockSpec((1, 1, d, tf), lambda i, f, te, tn: (j, te[i], 0, fidx(i, f, tn))),
                  pl.BlockSpec((1, 1, tf, d), lambda i, f, te, tn: (j, te[i], fidx(i, f, tn), 0))],
        out_specs=pl.BlockSpec((tm, d), lambda i, f, te, tn: (i, 0)),
    )
    return pl.pallas_call(
        functools.partial(_moe_kernel, sub=FFN_SUB),
        grid_spec=grid_spec,
        out_shape=jax.ShapeDtypeStruct((r, d), F32),
        compiler_params=_cparams(("arbitrary", "arbitrary"), 56),
        name="moe_experts",
    )(tile_expert, tile_rows, x_sorted, wg, wu, wd)


def _combine_kernel(h_ref, y0_ref, y1_ref, r_ref, mod_ref, o_ref):
    r = r_ref[...]
    mixed = r[:, 2:3] * y0_ref[...] + r[:, 3:4] * y1_ref[...]
    o_ref[...] = h_ref[...] + mod_ref[0][5:6, :] * mixed


def _combine(h, y0, y1, route, mod, rows_per_mod):
    t, d = h.shape
    tm = 512
    tiles_per_mod = rows_per_mod // tm
    row = lambda i: (i, 0)
    return pl.pallas_call(
        _combine_kernel,
        grid=(t // tm,),
        in_specs=[pl.BlockSpec((tm, d), row), pl.BlockSpec((tm, d), row), pl.BlockSpec((tm, d), row),
                  pl.BlockSpec((tm, LANES), row),
                  pl.BlockSpec((1, 6, d), lambda i: (i // tiles_per_mod, 0, 0))],
        out_specs=pl.BlockSpec((tm, d), row),
        out_shape=jax.ShapeDtypeStruct((t, d), F32),
        compiler_params=_cparams(("parallel",), 48),
        name="moe_combine",
    )(h, y0, y1, route, mod)


def _moe(v, route, h, mod, wg, wu, wd, j, rows_per_mod):
    t = v.shape[0]
    tm = FFN_TM
    n_assign = t * TOP_K
    rows = n_assign + N_EXPERTS * tm
    expert = route[:, :TOP_K].astype(jnp.int32).reshape(-1)
    onehot = (expert[:, None] == jnp.arange(N_EXPERTS, dtype=jnp.int32)[None, :]).astype(jnp.int32)
    csum = jnp.cumsum(onehot, axis=0)
    rank = jnp.sum(csum * onehot, axis=1) - 1
    counts = csum[-1]
    padded = ((counts + tm - 1) // tm) * tm
    ends = jnp.cumsum(padded)
    starts = ends - padded
    dest = starts[expert] + rank
    row_token = jnp.zeros((rows,), jnp.int32).at[dest].set(jnp.arange(n_assign, dtype=jnp.int32) // TOP_K)
    tile_start = jnp.arange(rows // tm, dtype=jnp.int32) * tm
    tile_expert = jnp.minimum(jnp.searchsorted(ends, tile_start, side="right"), N_EXPERTS - 1).astype(jnp.int32)
    tile_rows = jnp.clip((starts + counts)[tile_expert] - tile_start, 0, tm)
    tile_rows = jnp.where(tile_start < ends[-1], tile_rows, 0).astype(jnp.int32)
    last_valid = jnp.maximum(ends[-1] // tm - 1, 0)
    tile_expert = jnp.where(tile_rows > 0, tile_expert, tile_expert[last_valid])
    x_sorted = v.at[row_token].get(mode="promise_in_bounds")
    y_sorted = _moe_experts(tile_expert, tile_rows, x_sorted, wg, wu, wd, j)
    dest2 = dest.reshape(t, TOP_K)
    y0 = jnp.take(y_sorted, dest2[:, 0], axis=0)
    y1 = jnp.take(y_sorted, dest2[:, 1], axis=0)
    return _combine(h, y0, y1, route, mod, rows_per_mod)


def _pad_cols(a, n):
    return jnp.pad(a, ((0, 0), (0, n - a.shape[1])))


def _prep_w_in(w):
    z, xbc, dtr, pool, cq, ckv, kr = jnp.split(w, np.cumsum(IN_SIZES)[:-1].tolist(), axis=1)
    parts = [z, xbc, pool, cq, ckv, _pad_cols(kr, LANES), _pad_cols(dtr[:, :SSD_HEADS], LANES),
             _pad_cols(dtr[:, SSD_HEADS:], LANES)]
    return jnp.concatenate(parts, axis=1).astype(BF16)


def _prep_w_uq(w):
    w = w.reshape(Q_LORA, MLA_HEADS, MLA_QK)
    w = jnp.pad(w, ((0, 0), (0, 0), (0, MLA_QKP - MLA_QK)))
    return w.reshape(Q_LORA, MLA_HEADS * MLA_QKP).astype(BF16)


def _prep_w_ukv(w):
    w = w.reshape(KV_LORA, MLA_HEADS, MLA_NOPE + MLA_V)
    k = w[:, :, :MLA_NOPE].reshape(KV_LORA, MLA_HEADS * MLA_NOPE)
    v = w[:, :, MLA_NOPE:].reshape(KV_LORA, MLA_HEADS * MLA_V)
    return jnp.concatenate([k, v], axis=1).astype(BF16)


def _rope_tables(seq):
    quarter = MLA_ROPE // 4
    inv = ROPE_BASE ** (-jnp.arange(quarter, dtype=F32) / quarter)
    pos = jnp.arange(seq, dtype=jnp.int32)
    pos_row = (pos // GRID_W).astype(F32)
    pos_col = (pos % GRID_W).astype(F32)
    ang_row = pos_row[:, None] * inv[None, :]
    ang_col = pos_col[:, None] * inv[None, :]
    ang = jnp.concatenate([ang_row, ang_row, ang_col, ang_col], axis=1)
    sign = jnp.tile(jnp.concatenate([-jnp.ones((quarter,), F32), jnp.ones((quarter,), F32)]), 2)
    cos = jnp.concatenate([jnp.cos(ang), jnp.ones((seq, LANES - MLA_ROPE), F32)], axis=1)
    sin = jnp.concatenate([jnp.sin(ang) * sign[None, :], jnp.zeros((seq, LANES - MLA_ROPE), F32)], axis=1)
    return cos, sin


def _head_expand():
    k = np.arange(LANES)[:, None]
    j = np.arange(SSD_W)[None, :]
    return jnp.asarray((k == j // SSD_HEADDIM).astype(np.float32), dtype=BF16)


def _split_hi_lo(w):
    hi = w.astype(BF16)
    lo = (w - hi.astype(F32)).astype(BF16)
    return hi, lo


def kernel(x, c, ctx, c_ctx, ada_w, ada_b, norm_mix_w, norm_ffn_w, w_in, ssd_conv_w, ssd_conv_b, ssd_dt_bias, ssd_a_log, ssd_d, ssd_norm_w, pool_w, pool_scale, mla_q_norm_w, mla_w_uq, mla_kv_norm_w, mla_w_ukv, mla_q_head_norm_w, mla_k_head_norm_w, w_out, ffn_w_gate, ffn_w_up, ffn_w_down, moe_w_router, moe_w_gate, moe_w_up, moe_w_down):
    bsz, seq, d = x.shape
    lc = ctx.shape[1]
    depth = ada_w.shape[0]
    h_l = x.reshape(bsz * seq, d)
    h_c = ctx.reshape(bsz * lc, d)
    t_l, t_c = h_l.shape[0], h_c.shape[0]

    pad_rows = (-(bsz + 1)) % SUBLANES
    cvec = jnp.concatenate([c, c_ctx[None, :], jnp.zeros((pad_rows, d), F32)], axis=0)
    rope_tabs = _rope_tables(seq)
    expand = _head_expand()
    zero_state = jnp.zeros((bsz, 2, SSD_GROUPS, SSD_STATE, SSD_GW), F32)

    for i in range(depth):
        last = i == depth - 1
        j = i // 2
        mod = _ada(cvec, ada_w[i], ada_b[i][None, :])
        mod_l = mod[:bsz].reshape(bsz, 6, d)
        mod_c = mod[bsz:bsz + 1].reshape(1, 6, d)

        w_in_i = _prep_w_in(w_in[i])
        nmw = norm_mix_w[i][None, :]
        z_l, xbc_l, pool_l, cq_l, ckv_l, kr_l, dt_l = _in_proj(h_l, mod_l, nmw, w_in_i, seq)
        z_c, xbc_c, pool_c, cq_c, ckv_c, kr_c, dt_c = _in_proj(h_c, mod_c, nmw, w_in_i, t_c)

        conv_b = ssd_conv_b[i][None, :]
        dtb = _pad_cols(ssd_dt_bias[i], LANES)[:, None, :]
        alog = _pad_cols(ssd_a_log[i], LANES)[:, None, :]
        dsk = jnp.repeat(ssd_d[i], SSD_HEADDIM)[None, :]
        snw = ssd_norm_w[i][None, :]
        ssd_args = (ssd_conv_w[i], conv_b, dtb, alog, dsk, snw, expand, bsz)
        ssd_c, st_c = _ssd(xbc_c, z_c, dt_c, zero_state, *ssd_args)
        ssd_l, _ = _ssd(xbc_l, z_l, dt_l, st_c, *ssd_args)

        pw = pool_w[i].astype(BF16)
        psc = pool_scale[i][None, :]
        pool_out_l = _pool(pool_l, pw, psc, bsz)

        qhn = _pad_cols(mla_q_head_norm_w[i][None, :], MLA_QKP)
        khn = _pad_cols(mla_k_head_norm_w[i][None, :], MLA_QKP)
        mla_args = (mla_q_norm_w[i][None, :], _prep_w_uq(mla_w_uq[i]), mla_kv_norm_w[i][None, :],
                    _prep_w_ukv(mla_w_ukv[i]), qhn, khn)
        q_l, k_l, v_l = _mla_proj(cq_l, ckv_l, kr_l, rope_tabs, *mla_args, with_q=True)
        if last:
            k_c, v_c = _mla_proj(cq_c, ckv_c, kr_c, None, *mla_args, with_q=False)
        else:
            q_c, k_c, v_c = _mla_proj(cq_c, ckv_c, kr_c, None, *mla_args, with_q=True)
        att_l = _attention(q_l, k_c, v_c, k_l, v_l, bsz)

        w_out_i = w_out[i].astype(BF16)
        nfw = norm_ffn_w[i][None, :]
        dense = i % 2 == 0
        router = None if dense else _split_hi_lo(_pad_cols(moe_w_router[j], LANES))
        res_l = _out_proj(ssd_l, pool_out_l, att_l, h_l, mod_l, nfw, w_out_i, seq, router)
        if dense:
            wg, wu, wd = ffn_w_gate, ffn_w_up, ffn_w_down
            h_l = _ffn(res_l[1], wg, wu, wd, j, res_l[0], mod_l, seq)
        else:
            wg, wu, wd = moe_w_gate, moe_w_up, moe_w_down
            h_l = _moe(res_l[1], res_l[2], res_l[0], mod_l, wg, wu, wd, j, seq)
        if not last:
            pool_out_c = _pool(pool_c, pw, psc, bsz)
            att_c = _attention(q_c, k_c, v_c, None, None, bsz)
            res_c = _out_proj(ssd_c, pool_out_c, att_c, h_c, mod_c, nfw, w_out_i, t_c, router)
            if dense:
                h_c = _ffn(res_c[1], wg, wu, wd, j, res_c[0], mod_c, t_c)
            else:
                h_c = _moe(res_c[1], res_c[2], res_c[0], mod_c, wg, wu, wd, j, t_c)
    return h_l.reshape(bsz, seq, d)
```

```python
import functools

import numpy as np
import jax
import jax.numpy as jnp
from jax import lax
from jax.experimental import pallas as pl
from jax.experimental.pallas import tpu as pltpu

F32 = jnp.float32
BF16 = jnp.bfloat16

D_MODEL = 2048
GRID_W = 64
SSD_HEADS = 12
SSD_HEADDIM = 64
SSD_W = SSD_HEADS * SSD_HEADDIM
SSD_GROUPS = 2
SSD_HPG = SSD_HEADS // SSD_GROUPS
SSD_STATE = 128
SSD_CONV = 4
SSD_CHUNK = 128
SSD_BC = SSD_GROUPS * SSD_STATE
SSD_CONV_CH = SSD_W + 2 * SSD_BC
SSD_GW = SSD_HPG * SSD_HEADDIM
POOL_WINDOWS = (2, 4, 8, 16)
POOL_GROUPS = 4
POOL_GC = 128
POOL_W = POOL_GROUPS * POOL_GC
POOL_HALO = 8
MLA_HEADS = 6
MLA_NOPE = 128
MLA_ROPE = 64
MLA_QK = MLA_NOPE + MLA_ROPE
MLA_QKP = 256
MLA_V = 128
MLA_W = MLA_HEADS * MLA_V
Q_LORA = 512
KV_LORA = 512
ROPE_BASE = 10000.0
LOG2_E = 1.4426950408889634
FFN_DIM = 5632
N_EXPERTS = 8
TOP_K = 2
NORM_EPS = 1e-6
IN_SIZES = (SSD_W, SSD_CONV_CH, 2 * SSD_HEADS, POOL_W, Q_LORA, KV_LORA, MLA_ROPE)
LANES = 128
SUBLANES = 8
IN_Z = 0
IN_XBC = IN_Z + SSD_W
IN_POOL = IN_XBC + SSD_CONV_CH
IN_CQ = IN_POOL + POOL_W
IN_CKV = IN_CQ + Q_LORA
IN_KR = IN_CKV + KV_LORA
IN_DT0 = IN_KR + LANES
IN_DT1 = IN_DT0 + LANES
IN_COLS = IN_DT1 + LANES
MIB = 1024 * 1024


def _cparams(sem, vmem_mib):
    return pltpu.CompilerParams(dimension_semantics=sem, vmem_limit_bytes=vmem_mib * MIB)


def _sigmoid(x):
    return 1.0 / (1.0 + jnp.exp(-x))


def _silu(x):
    return x * _sigmoid(x)


def _softplus(x):
    return jnp.maximum(x, 0.0) + jnp.log1p(jnp.exp(-jnp.abs(x)))


def _rms(x, w):
    return x * lax.rsqrt(jnp.mean(x * x, axis=-1, keepdims=True) + NORM_EPS) * w


def _bdot(a, b):
    return jnp.dot(a, b, preferred_element_type=F32)


def _split_bf16(x, passes):
    parts = []
    r = x
    for _ in range(passes):
        p = r.astype(BF16)
        parts.append(p)
        r = r - p.astype(F32)
    return parts


def _ldot(m_bf16, x, passes):
    out = None
    for p in _split_bf16(x, passes):
        t = _bdot(m_bf16, p)
        out = t if out is None else out + t
    return out


def _rdot(x, m_bf16, passes):
    out = None
    for p in _split_bf16(x, passes):
        t = _bdot(p, m_bf16)
        out = t if out is None else out + t
    return out


def _ada_kernel(c_ref, w_ref, b_ref, o_ref):
    s = _silu(c_ref[...]).astype(BF16)
    o_ref[...] = _bdot(s, w_ref[...].astype(BF16)) + b_ref[...]


def _ada(cvec, w, b):
    rows, d = cvec.shape
    n = w.shape[1]
    tn = 1024
    return pl.pallas_call(
        _ada_kernel,
        grid=(n // tn,),
        in_specs=[pl.BlockSpec((rows, d), lambda j: (0, 0)),
                  pl.BlockSpec((d, tn), lambda j: (0, j)),
                  pl.BlockSpec((1, tn), lambda j: (0, j))],
        out_specs=pl.BlockSpec((rows, tn), lambda j: (0, j)),
        out_shape=jax.ShapeDtypeStruct((rows, n), F32),
        compiler_params=_cparams(("arbitrary",), 40),
        name="ada",
    )(cvec, w, b)


def _in_proj_kernel(h_ref, mod_ref, nw_ref, w_ref, z_ref, xbc_ref, pool_ref, cq_ref, ckv_ref,
                    kr_ref, dt_ref):
    mod = mod_ref[0]
    u = _rms(h_ref[...], nw_ref[...]) * (1.0 + mod[1:2, :]) + mod[0:1, :]
    ub = u.astype(BF16)
    z_ref[...] = _bdot(ub, w_ref[:, IN_Z:IN_XBC])
    xbc_ref[...] = _bdot(ub, w_ref[:, IN_XBC:IN_POOL])
    pool_ref[...] = _bdot(ub, w_ref[:, IN_POOL:IN_CQ])
    cq_ref[...] = _bdot(ub, w_ref[:, IN_CQ:IN_CKV])
    ckv_ref[...] = _bdot(ub, w_ref[:, IN_CKV:IN_KR])
    kr_ref[...] = _bdot(ub, w_ref[:, IN_KR:IN_DT0])
    dt_ref[0] = _bdot(ub, w_ref[:, IN_DT0:IN_DT1])
    dt_ref[1] = _bdot(ub, w_ref[:, IN_DT1:IN_COLS])


def _in_proj(h, mod, norm_w, w, rows_per_mod):
    t, d = h.shape
    tm = 256
    tiles_per_mod = rows_per_mod // tm
    widths = (SSD_W, SSD_CONV_CH, POOL_W, Q_LORA, KV_LORA, LANES)
    row = lambda i: (i, 0)
    return pl.pallas_call(
        _in_proj_kernel,
        grid=(t // tm,),
        in_specs=[pl.BlockSpec((tm, d), row),
                  pl.BlockSpec((1, 6, d), lambda i: (i // tiles_per_mod, 0, 0)),
                  pl.BlockSpec((1, d), lambda i: (0, 0)),
                  pl.BlockSpec((d, IN_COLS), lambda i: (0, 0))],
        out_specs=[pl.BlockSpec((tm, wd), row) for wd in widths]
        + [pl.BlockSpec((2, tm, LANES), lambda i: (0, i, 0))],
        out_shape=[jax.ShapeDtypeStruct((t, wd), F32) for wd in widths]
        + [jax.ShapeDtypeStruct((2, t, LANES), F32)],
        compiler_params=_cparams(("parallel",), 56),
        name="in_proj",
    )(h, mod, norm_w, w)


def _ssd_kernel(xm_ref, xp_ref, xn_ref, z_ref, dt_ref, init_ref, cw_ref, cb_ref, dtb_ref, alog_ref,
                dsk_ref, nw_ref, e_ref, y_ref, fin_ref, yf_scr, st_scr, xe_scr, xbc_scr, *, nc):
    q = SSD_CHUNK
    t = pl.program_id(1)
    d = t // nc
    cpos = t % nc
    c = jnp.where(d == 0, cpos, nc - 1 - cpos)
    r0 = pl.multiple_of(c * q, q)

    @pl.when(cpos == 0)
    def _():
        st_scr[...] = init_ref[0, 0]

    @pl.when(d == 0)
    def _():
        xe_scr[0:SUBLANES, :] = jnp.where(c == 0, 0.0, xp_ref[...])
        xe_scr[SUBLANES:SUBLANES + q, :] = xm_ref[...]
        xe_scr[SUBLANES + q:, :] = jnp.where(c == nc - 1, 0.0, xn_ref[...])
        cw = cw_ref[...]
        off = SUBLANES - SSD_CONV // 2
        acc = cb_ref[...]
        for k in range(SSD_CONV):
            acc = acc + cw[k:k + 1, :] * xe_scr[off + k:off + k + q, :]
        xbc_scr[pl.ds(r0, q), :] = _silu(acc)

    xbc = xbc_scr[pl.ds(r0, q), :]
    xs = xbc[:, :SSD_W]

    lane = lax.broadcasted_iota(jnp.int32, (q, LANES), 1)
    row = lax.broadcasted_iota(jnp.int32, (q, LANES), 0)
    dt = jnp.where(lane < SSD_HEADS, _softplus(dt_ref[0] + dtb_ref[0]), 0.0)
    adt = dt * (-jnp.exp(alog_ref[0]))
    causal = (row - lane) * (1 - 2 * d) >= 0
    tri = jnp.where(causal, 1.0, 0.0).astype(BF16)
    cs = _ldot(tri, adt, 3)
    tot = jnp.sum(adt, axis=0, keepdims=True)
    e = e_ref[...]
    dtx = _rdot(dt, e, 2)
    ecsx = _rdot(jnp.exp(cs), e, 2)
    wx = _rdot(dt * jnp.exp(tot - cs), e, 2)
    etot = _rdot(jnp.broadcast_to(jnp.exp(tot), (SUBLANES, LANES)), e, 2)[0:1, :]
    xdt = xs * dtx
    xw = (xs * wx).astype(BF16)
    cs_t = cs.T
    low_half = lane < SSD_HEADDIM

    ys = []
    for g in range(SSD_GROUPS):
        bg = xbc[:, SSD_W + g * SSD_STATE:SSD_W + (g + 1) * SSD_STATE]
        cg = xbc[:, SSD_W + SSD_BC + g * SSD_STATE:SSD_W + SSD_BC + (g + 1) * SSD_STATE]
        cgb = cg.astype(BF16)
        cb = lax.dot_general(cgb, bg.astype(BF16), (((1,), (1,)), ((), ())),
                             preferred_element_type=F32)
        gs = slice(g * SSD_GW, (g + 1) * SSD_GW)
        st = st_scr[g]
        y_off = _bdot(cgb, st.astype(BF16)) * ecsx[:, gs]
        st_scr[g] = st * etot[:, gs] + _bdot(bg.T.astype(BF16), xw[:, gs])
        for pr in range(SSD_HPG // 2):
            xpair = xdt[:, g * SSD_GW + pr * LANES:g * SSD_GW + (pr + 1) * LANES]
            y_pair = y_off[:, pr * LANES:(pr + 1) * LANES]
            for k in range(2):
                h = g * SSD_HPG + pr * 2 + k
                dcs = cs[:, h:h + 1] - cs_t[h:h + 1, :]
                m = (cb * jnp.exp(jnp.where(causal, dcs, -1e30))).astype(BF16)
                xh = jnp.where(low_half if k == 0 else jnp.logical_not(low_half), xpair, 0.0)
                y_pair = y_pair + _bdot(m, xh.astype(BF16))
            ys.append(y_pair)
    y = jnp.concatenate(ys, axis=1)

    @pl.when(d == 0)
    def _():
        yf_scr[pl.ds(r0, q), :] = y

    @pl.when(d == 1)
    def _():
        g_ = (yf_scr[pl.ds(r0, q), :] + y + xs * dsk_ref[...]) * _silu(z_ref[...])
        y_ref[...] = _rms(g_, nw_ref[...]).astype(BF16)

    @pl.when(cpos == nc - 1)
    def _():
        fin_ref[0, 0] = st_scr[...]


def _ssd(xbc, z, dt, init, conv_w, conv_b, dt_bias, a_log, d_skip, norm_w, expand, bsz):
    t = xbc.shape[0]
    q = SSD_CHUNK
    nc = t // bsz // q
    tiles8 = q // SUBLANES

    def chunk(tt):
        cp = tt % nc
        return jnp.where(tt < nc, cp, nc - 1 - cp)

    def out_chunk(tt):
        return jnp.where(tt < nc, nc - 1, 2 * nc - 1 - tt)

    def conv_chunk(tt):
        return jnp.minimum(tt, nc - 1)

    const2 = lambda b, tt: (0, 0)
    state_spec = pl.BlockSpec((1, 1, SSD_GROUPS, SSD_STATE, SSD_GW), lambda b, tt: (b, tt // nc, 0, 0, 0))
    state_shape = jax.ShapeDtypeStruct((bsz, 2, SSD_GROUPS, SSD_STATE, SSD_GW), F32)
    return pl.pallas_call(
        functools.partial(_ssd_kernel, nc=nc),
        grid=(bsz, 2 * nc),
        in_specs=[
            pl.BlockSpec((q, SSD_CONV_CH), lambda b, tt: (b * nc + conv_chunk(tt), 0)),
            pl.BlockSpec((SUBLANES, SSD_CONV_CH),
                         lambda b, tt: (jnp.maximum((b * nc + conv_chunk(tt)) * tiles8 - 1, 0), 0)),
            pl.BlockSpec((SUBLANES, SSD_CONV_CH),
                         lambda b, tt: (jnp.minimum((b * nc + conv_chunk(tt) + 1) * tiles8, t // SUBLANES - 1), 0)),
            pl.BlockSpec((q, SSD_W), lambda b, tt: (b * nc + out_chunk(tt), 0)),
            pl.BlockSpec((1, q, LANES), lambda b, tt: (tt // nc, b * nc + chunk(tt), 0)),
            state_spec,
            pl.BlockSpec((SSD_CONV, SSD_CONV_CH), const2),
            pl.BlockSpec((1, SSD_CONV_CH), const2),
            pl.BlockSpec((1, 1, LANES), lambda b, tt: (tt // nc, 0, 0)),
            pl.BlockSpec((1, 1, LANES), lambda b, tt: (tt // nc, 0, 0)),
            pl.BlockSpec((1, SSD_W), const2),
            pl.BlockSpec((1, SSD_W), const2),
            pl.BlockSpec((LANES, SSD_W), const2),
        ],
        out_specs=[pl.BlockSpec((q, SSD_W), lambda b, tt: (b * nc + out_chunk(tt), 0)), state_spec],
        out_shape=[jax.ShapeDtypeStruct((t, SSD_W), BF16), state_shape],
        scratch_shapes=[pltpu.VMEM((nc * q, SSD_W), F32),
                        pltpu.VMEM((SSD_GROUPS, SSD_STATE, SSD_GW), F32),
                        pltpu.VMEM((q + 2 * SUBLANES, SSD_CONV_CH), F32),
                        pltpu.VMEM((nc * q, SSD_CONV_CH), F32)],
        compiler_params=_cparams(("parallel", "arbitrary"), 48),
        name="ssd",
    )(xbc, xbc, xbc, z, dt, init, conv_w, conv_b, dt_bias, a_log, d_skip, norm_w, expand)


def _pool_kernel(um_ref, up_ref, un_ref, pw_ref, ps_ref, o_ref, pe_scr, *, seq):
    tp = um_ref.shape[0]
    j = pl.program_id(1)
    pe_scr[0:POOL_HALO, :] = jnp.where(j == 0, 0.0, up_ref[...])
    pe_scr[POOL_HALO:POOL_HALO + tp, :] = um_ref[...]
    pe_scr[POOL_HALO + tp:, :] = jnp.where(j == pl.num_programs(1) - 1, 0.0, un_ref[...])
    pos = j * tp + lax.broadcasted_iota(jnp.int32, (tp, 1), 0)
    for g, w in enumerate(POOL_WINDOWS):
        cols = slice(g * POOL_GC, (g + 1) * POOL_GC)
        start = POOL_HALO - w // 2
        win = pe_scr[start:start + tp, cols]
        for k in range(1, w):
            win = win + pe_scr[start + k:start + k + tp, cols]
        lo = jnp.maximum(pos - w // 2, 0)
        hi = jnp.minimum(pos - w // 2 + w, seq)
        pooled = win / (hi - lo).astype(F32) - um_ref[:, cols]
        o_ref[:, cols] = (_bdot(pooled.astype(BF16), pw_ref[g]) * ps_ref[:, cols]).astype(BF16)


def _pool(u, pool_w, pool_scale, bsz):
    t = u.shape[0]
    seq = t // bsz
    tp = 256
    nt = seq // tp
    tiles8 = tp // SUBLANES
    return pl.pallas_call(
        functools.partial(_pool_kernel, seq=seq),
        grid=(bsz, nt),
        in_specs=[
            pl.BlockSpec((tp, POOL_W), lambda b, j: (b * nt + j, 0)),
            pl.BlockSpec((SUBLANES, POOL_W), lambda b, j: (jnp.maximum((b * nt + j) * tiles8 - 1, 0), 0)),
            pl.BlockSpec((SUBLANES, POOL_W),
                         lambda b, j: (jnp.minimum((b * nt + j + 1) * tiles8, t // SUBLANES - 1), 0)),
            pl.BlockSpec((POOL_GROUPS, POOL_GC, POOL_GC), lambda b, j: (0, 0, 0)),
            pl.BlockSpec((1, POOL_W), lambda b, j: (0, 0)),
        ],
        out_specs=pl.BlockSpec((tp, POOL_W), lambda b, j: (b * nt + j, 0)),
        out_shape=jax.ShapeDtypeStruct((t, POOL_W), BF16),
        scratch_shapes=[pltpu.VMEM((tp + 2 * POOL_HALO, POOL_W), F32)],
        compiler_params=_cparams(("parallel", "parallel"), 32),
        name="pool",
    )(u, u, u, pool_w, pool_scale)


def _rope(r, cos, sin_signed):
    lane = lax.broadcasted_iota(jnp.int32, r.shape, 1)
    quarter = MLA_ROPE // 4
    swapped = jnp.where(lane % (2 * quarter) < quarter,
                        pltpu.roll(r, LANES - quarter, 1), pltpu.roll(r, quarter, 1))
    return r * cos + swapped * sin_signed


def _mla_kernel(*refs, use_rope, with_q):
    if use_rope:
        cq_ref, ckv_ref, kr_ref, cos_ref, sin_ref = refs[:5]
        refs = refs[5:]
    else:
        cq_ref, ckv_ref, kr_ref = refs[:3]
        refs = refs[3:]
    qn_ref, wq_ref, kvn_ref, wkv_ref, qhn_ref, khn_ref = refs[:6]
    outs = refs[6:]
    if with_q:
        q_ref, k_ref, v_ref = outs
    else:
        k_ref, v_ref = outs
    rope = (lambda r: _rope(r, cos_ref[...], sin_ref[...])) if use_rope else (lambda r: r)
    inv_qk = 1.0 / MLA_QK

    if with_q:
        qf = _bdot(_rms(cq_ref[...], qn_ref[...]).astype(BF16), wq_ref[...])
        scale = MLA_QK ** -0.5 * LOG2_E
        for h in range(MLA_HEADS):
            qh = qf[:, h * MLA_QKP:(h + 1) * MLA_QKP]
            rs = lax.rsqrt(jnp.sum(qh * qh, axis=-1, keepdims=True) * inv_qk + NORM_EPS) * scale
            qh = qh * rs * qhn_ref[...]
            q_ref[:, h * MLA_QKP:h * MLA_QKP + MLA_NOPE] = qh[:, :MLA_NOPE].astype(BF16)
            q_ref[:, h * MLA_QKP + MLA_NOPE:(h + 1) * MLA_QKP] = rope(qh[:, MLA_NOPE:]).astype(BF16)

    kvf = _bdot(_rms(ckv_ref[...], kvn_ref[...]).astype(BF16), wkv_ref[...])
    kr = kr_ref[...]
    ss_kr = jnp.sum(kr * kr, axis=-1, keepdims=True)
    kr_w = rope(kr * khn_ref[:, MLA_NOPE:])
    for h in range(MLA_HEADS):
        kn = kvf[:, h * MLA_NOPE:(h + 1) * MLA_NOPE]
        rs = lax.rsqrt((jnp.sum(kn * kn, axis=-1, keepdims=True) + ss_kr) * inv_qk + NORM_EPS)
        k_ref[h * MLA_QKP:h * MLA_QKP + MLA_NOPE, :] = (kn * rs * khn_ref[:, :MLA_NOPE]).T.astype(BF16)
        k_ref[h * MLA_QKP + MLA_NOPE:(h + 1) * MLA_QKP, :] = (kr_w * rs).T.astype(BF16)
    v_ref[...] = kvf[:, MLA_HEADS * MLA_NOPE:].astype(BF16)


def _mla_proj(cq, ckv, kr, rope_tabs, qn, wq, kvn, wkv, qhn, khn, with_q):
    t = cq.shape[0]
    tm = 256
    use_rope = rope_tabs is not None
    row = lambda i: (i, 0)
    const = lambda i: (0, 0)
    ins = [cq, ckv, kr]
    in_specs = [pl.BlockSpec((tm, Q_LORA), row), pl.BlockSpec((tm, KV_LORA), row), pl.BlockSpec((tm, LANES), row)]
    if use_rope:
        cos, sin = rope_tabs
        nrt = cos.shape[0] // tm
        ins += [cos, sin]
        in_specs += [pl.BlockSpec((tm, LANES), lambda i: (i % nrt, 0))] * 2
    ins += [qn, wq, kvn, wkv, qhn, khn]
    in_specs += [pl.BlockSpec((1, Q_LORA), const), pl.BlockSpec((Q_LORA, MLA_HEADS * MLA_QKP), const),
                 pl.BlockSpec((1, KV_LORA), const), pl.BlockSpec((KV_LORA, MLA_HEADS * (MLA_NOPE + MLA_V)), const),
                 pl.BlockSpec((1, MLA_QKP), const), pl.BlockSpec((1, MLA_QKP), const)]
    out_specs = [pl.BlockSpec((MLA_HEADS * MLA_QKP, tm), lambda i: (0, i)), pl.BlockSpec((tm, MLA_W), row)]
    out_shape = [jax.ShapeDtypeStruct((MLA_HEADS * MLA_QKP, t), BF16), jax.ShapeDtypeStruct((t, MLA_W), BF16)]
    if with_q:
        out_specs = [pl.BlockSpec((tm, MLA_HEADS * MLA_QKP), row)] + out_specs
        out_shape = [jax.ShapeDtypeStruct((t, MLA_HEADS * MLA_QKP), BF16)] + out_shape
    return pl.pallas_call(
        functools.partial(_mla_kernel, use_rope=use_rope, with_q=with_q),
        grid=(t // tm,),
        in_specs=in_specs,
        out_specs=out_specs,
        out_shape=out_shape,
        compiler_params=_cparams(("parallel",), 40),
        name="mla_proj",
    )(*ins)


ATTN_CHAIN_ROWS = 256
ATTN_CHAINS = 2


def _attn_kernel(*refs, latent_keys, chains):
    if latent_keys:
        q_ref, kc_ref, vc_ref, kl_ref, vl_ref, o_ref = refs
    else:
        q_ref, kc_ref, vc_ref, o_ref = refs
    sub = ATTN_CHAIN_ROWS

    def body(it, carry):
        for ch in range(chains):
            rows = pl.ds(pl.multiple_of((it * chains + ch) * sub, sub), sub)
            q = q_ref[rows, :]
            sc = _bdot(q, kc_ref[...])
            m = jnp.max(sc, axis=-1, keepdims=True)
            if latent_keys:
                sl = _bdot(q, kl_ref[...])
                m = jnp.maximum(m, jnp.max(sl, axis=-1, keepdims=True))
            pc = jnp.exp2(sc - m)
            den = jnp.sum(pc, axis=-1, keepdims=True)
            o = _bdot(pc.astype(BF16), vc_ref[...])
            if latent_keys:
                pl_ = jnp.exp2(sl - m)
                den = den + jnp.sum(pl_, axis=-1, keepdims=True)
                o = o + _bdot(pl_.astype(BF16), vl_ref[...])
            o_ref[rows, :] = (o / den).astype(BF16)
        return carry

    lax.fori_loop(0, q_ref.shape[0] // (chains * sub), body, 0)


def _attention(q, kc, vc, kl, vl, bsz):
    tq_total = q.shape[0]
    lq = tq_total // bsz
    lc = kc.shape[1] // bsz
    chains = min(ATTN_CHAINS, lq // ATTN_CHAIN_ROWS)
    latent_keys = kl is not None
    ins = [q, kc, vc]
    in_specs = [pl.BlockSpec((lq, MLA_QKP), lambda b, h: (b, h)),
                pl.BlockSpec((MLA_QKP, lc), lambda b, h: (h, b)),
                pl.BlockSpec((lc, MLA_V), lambda b, h: (b, h))]
    if latent_keys:
        ll = kl.shape[1] // bsz
        ins += [kl, vl]
        in_specs += [pl.BlockSpec((MLA_QKP, ll), lambda b, h: (h, b)),
                     pl.BlockSpec((ll, MLA_V), lambda b, h: (b, h))]
    return pl.pallas_call(
        functools.partial(_attn_kernel, latent_keys=latent_keys, chains=chains),
        grid=(bsz, MLA_HEADS),
        in_specs=in_specs,
        out_specs=pl.BlockSpec((lq, MLA_V), lambda b, h: (b, h)),
        out_shape=jax.ShapeDtypeStruct((tq_total, MLA_W), BF16),
        compiler_params=_cparams(("parallel", "parallel"), 48),
        name="attention",
    )(*ins)


def _out_proj_kernel(*refs, route):
    ssd_ref, pool_ref, att_ref, h_ref, mod_ref, nw_ref, w_ref = refs[:7]
    refs = refs[7:]
    if route:
        wr_hi_ref, wr_lo_ref, ho_ref, v_ref, r_ref = refs
    else:
        ho_ref, v_ref = refs
    mod = mod_ref[0]
    for r0 in range(0, h_ref.shape[0], OUT_SUB):
        rows = slice(r0, r0 + OUT_SUB)
        mix = (_bdot(ssd_ref[rows, :], w_ref[0:SSD_W, :])
               + _bdot(pool_ref[rows, :], w_ref[SSD_W:SSD_W + POOL_W, :])
               + _bdot(att_ref[rows, :], w_ref[SSD_W + POOL_W:, :]))
        hn = h_ref[rows, :] + mod[2:3, :] * mix
        ho_ref[rows, :] = hn
        v = _rms(hn, nw_ref[...]) * (1.0 + mod[4:5, :]) + mod[3:4, :]
        vb = v.astype(BF16)
        v_ref[rows, :] = vb
        if route:
            v_lo = (v - vb.astype(F32)).astype(BF16)
            logits = _bdot(vb, wr_hi_ref[...]) + _bdot(v_lo, wr_hi_ref[...]) + _bdot(vb, wr_lo_ref[...])
            lane = lax.broadcasted_iota(jnp.int32, logits.shape, 1).astype(F32)
            neg = -jnp.inf
            logits = jnp.where(lane < N_EXPERTS, logits, neg)
            m1 = jnp.max(logits, axis=-1, keepdims=True)
            i1 = jnp.min(jnp.where(logits == m1, lane, float(LANES)), axis=-1, keepdims=True)
            rest = jnp.where(lane == i1, neg, logits)
            m2 = jnp.max(rest, axis=-1, keepdims=True)
            i2 = jnp.min(jnp.where(rest == m2, lane, float(LANES)), axis=-1, keepdims=True)
            e2 = jnp.exp(m2 - m1)
            g1 = 1.0 / (1.0 + e2)
            g2 = e2 / (1.0 + e2)
            r_ref[rows, :] = jnp.where(lane == 0, i1, jnp.where(lane == 1, i2, jnp.where(lane == 2, g1, jnp.where(lane == 3, g2, 0.0))))


OUT_TM = 512
OUT_SUB = 256


def _out_proj(ssd, pool, att, h, mod, norm_w, w, rows_per_mod, router=None):
    t, d = h.shape
    tm = OUT_TM
    tiles_per_mod = rows_per_mod // tm
    row = lambda i: (i, 0)
    const = lambda i: (0, 0)
    route = router is not None
    ins = [ssd, pool, att, h, mod, norm_w, w]
    in_specs = [pl.BlockSpec((tm, SSD_W), row), pl.BlockSpec((tm, POOL_W), row), pl.BlockSpec((tm, MLA_W), row),
                pl.BlockSpec((tm, d), row), pl.BlockSpec((1, 6, d), lambda i: (i // tiles_per_mod, 0, 0)),
                pl.BlockSpec((1, d), const), pl.BlockSpec((d, d), const)]
    out_specs = [pl.BlockSpec((tm, d), row), pl.BlockSpec((tm, d), row)]
    out_shape = [jax.ShapeDtypeStruct((t, d), F32), jax.ShapeDtypeStruct((t, d), BF16)]
    if route:
        ins += list(router)
        in_specs += [pl.BlockSpec((d, LANES), const)] * 2
        out_specs.append(pl.BlockSpec((tm, LANES), row))
        out_shape.append(jax.ShapeDtypeStruct((t, LANES), F32))
    return pl.pallas_call(
        functools.partial(_out_proj_kernel, route=route),
        grid=(t // tm,),
        in_specs=in_specs,
        out_specs=out_specs,
        out_shape=out_shape,
        compiler_params=_cparams(("parallel",), 48),
        name="out_proj",
    )(*ins)


FFN_TM = 1024
FFN_TF = 256
MOE_TF = 512
FFN_SUB = 512


FFN_CAST_CHUNK = 512


def _swiglu_tile(x_ref, wg_ref, wu_ref, wd_ref, o_ref, row_blocks):
    d = x_ref.shape[1]
    ck = FFN_CAST_CHUNK
    g = [None] * len(row_blocks)
    u = [None] * len(row_blocks)
    for k0 in range(0, d, ck):
        wg = wg_ref[k0:k0 + ck, :].astype(BF16)
        wu = wu_ref[k0:k0 + ck, :].astype(BF16)
        for s, rows in enumerate(row_blocks):
            x = x_ref[rows, k0:k0 + ck]
            tg = _bdot(x, wg)
            tu = _bdot(x, wu)
            g[s] = tg if g[s] is None else g[s] + tg
            u[s] = tu if u[s] is None else u[s] + tu
    acts = [(_silu(g[s]) * u[s]).astype(BF16) for s in range(len(row_blocks))]
    for n0 in range(0, d, ck):
        wd = wd_ref[:, n0:n0 + ck].astype(BF16)
        for s, rows in enumerate(row_blocks):
            o_ref[rows, n0:n0 + ck] += _bdot(acts[s], wd)


def _ffn_kernel(v_ref, wg_ref, wu_ref, wd_ref, h_ref, mod_ref, o_ref, *, sub):
    f = pl.program_id(1)

    @pl.when(f == 0)
    def _():
        o_ref[...] = jnp.zeros_like(o_ref)

    row_blocks = [slice(s * sub, (s + 1) * sub) for s in range(v_ref.shape[0] // sub)]
    _swiglu_tile(v_ref, wg_ref.at[0], wu_ref.at[0], wd_ref.at[0], o_ref, row_blocks)

    @pl.when(f == pl.num_programs(1) - 1)
    def _():
        o_ref[...] = h_ref[...] + mod_ref[0][5:6, :] * o_ref[...]


def _ffn(v, wg, wu, wd, j, h, mod, rows_per_mod):
    t, d = h.shape
    tm = min(FFN_TM, rows_per_mod)
    tf = FFN_TF
    tiles_per_mod = rows_per_mod // tm
    nf = wg.shape[2] // tf
    return pl.pallas_call(
        functools.partial(_ffn_kernel, sub=min(FFN_SUB, tm)),
        grid=(t // tm, nf),
        in_specs=[pl.BlockSpec((tm, d), lambda i, f: (i, 0)),
                  pl.BlockSpec((1, d, tf), lambda i, f: (j, 0, f)),
                  pl.BlockSpec((1, d, tf), lambda i, f: (j, 0, f)),
                  pl.BlockSpec((1, tf, d), lambda i, f: (j, f, 0)),
                  pl.BlockSpec((tm, d), lambda i, f: (i, 0), pipeline_mode=pl.Buffered(1)),
                  pl.BlockSpec((1, 6, d), lambda i, f: (i // tiles_per_mod, 0, 0))],
        out_specs=pl.BlockSpec((tm, d), lambda i, f: (i, 0)),
        out_shape=jax.ShapeDtypeStruct((t, d), F32),
        compiler_params=_cparams(("parallel", "arbitrary"), 56),
        name="ffn",
    )(v, wg, wu, wd, h, mod)


def _moe_kernel(te_ref, tn_ref, x_ref, wg_ref, wu_ref, wd_ref, o_ref, acc_scr, *, sub):
    i = pl.program_id(0)
    f = pl.program_id(1)
    n_rows = tn_ref[i]

    @pl.when(n_rows > 0)
    def _():
        @pl.when(f == 0)
        def _():
            acc_scr[...] = jnp.zeros_like(acc_scr)

        refs = (x_ref, wg_ref.at[0, 0], wu_ref.at[0, 0], wd_ref.at[0, 0], acc_scr)
        n_blocks = x_ref.shape[0] // sub
        for live in range(1, n_blocks + 1):
            lo = (live - 1) * sub
            cond = (n_rows > lo) if live == n_blocks else jnp.logical_and(n_rows > lo, n_rows <= lo + sub)

            @pl.when(cond)
            def _():
                _swiglu_tile(*refs, [slice(s * sub, (s + 1) * sub) for s in range(live)])

        @pl.when(f == pl.num_programs(1) - 1)
        def _():
            o_ref[...] = acc_scr[...].astype(BF16)


def _moe_experts(tile_expert, tile_rows, x_sorted, wg, wu, wd, j):
    r, d = x_sorted.shape
    tm, tf = FFN_TM, MOE_TF
    nf = wg.shape[3] // tf

    def fidx(i, f, tn):
        return jnp.where(tn[i] > 0, f, nf - 1)

    grid_spec = pltpu.PrefetchScalarGridSpec(
        num_scalar_prefetch=2,
        grid=(r // tm, nf),
        in_specs=[pl.BlockSpec((tm, d), lambda i, f, te, tn: (i, 0)),
                  pl.BlockSpec((1, 1, d, tf), lambda i, f, te, tn: (j, te[i], 0, fidx(i, f, tn))),
                  pl.BlockSpec((1, 1, d, tf), lambda i, f, te, tn: (j, te[i], 0, fidx(i, f, tn))),
                  pl.BlockSpec((1, 1, tf, d), lambda i, f, te, tn: (j, te[i], fidx(i, f, tn), 0))],
        out_specs=pl.BlockSpec((tm, d), lambda i, f, te, tn: (i, 0)),
        scratch_shapes=[pltpu.VMEM((tm, d), F32)],
    )
    return pl.pallas_call(
        functools.partial(_moe_kernel, sub=FFN_SUB),
        grid_spec=grid_spec,
        out_shape=jax.ShapeDtypeStruct((r, d), BF16),
        compiler_params=_cparams(("arbitrary", "arbitrary"), 58),
        name="moe_experts",
    )(tile_expert, tile_rows, x_sorted, wg, wu, wd)


def _combine_kernel(h_ref, y0_ref, y1_ref, r_ref, mod_ref, o_ref):
    r = r_ref[...]
    mixed = r[:, 2:3] * y0_ref[...].astype(F32) + r[:, 3:4] * y1_ref[...].astype(F32)
    o_ref[...] = h_ref[...] + mod_ref[0][5:6, :] * mixed


def _combine(h, y0, y1, route, mod, rows_per_mod):
    t, d = h.shape
    tm = 512
    tiles_per_mod = rows_per_mod // tm
    row = lambda i: (i, 0)
    return pl.pallas_call(
        _combine_kernel,
        grid=(t // tm,),
        in_specs=[pl.BlockSpec((tm, d), row), pl.BlockSpec((tm, d), row), pl.BlockSpec((tm, d), row),
                  pl.BlockSpec((tm, LANES), row),
                  pl.BlockSpec((1, 6, d), lambda i: (i // tiles_per_mod, 0, 0))],
        out_specs=pl.BlockSpec((tm, d), row),
        out_shape=jax.ShapeDtypeStruct((t, d), F32),
        compiler_params=_cparams(("parallel",), 48),
        name="moe_combine",
    )(h, y0, y1, route, mod)


def _moe(v, route, h, mod, wg, wu, wd, j, rows_per_mod):
    t = v.shape[0]
    tm = FFN_TM
    n_assign = t * TOP_K
    rows = n_assign + N_EXPERTS * tm
    expert = route[:, :TOP_K].astype(jnp.int32).reshape(-1)
    onehot = (expert[:, None] == jnp.arange(N_EXPERTS, dtype=jnp.int32)[None, :]).astype(jnp.int32)
    csum = jnp.cumsum(onehot, axis=0)
    rank = jnp.sum(csum * onehot, axis=1) - 1
    counts = csum[-1]
    padded = ((counts + tm - 1) // tm) * tm
    ends = jnp.cumsum(padded)
    starts = ends - padded
    dest = starts[expert] + rank
    row_token = jnp.zeros((rows,), jnp.int32).at[dest].set(jnp.arange(n_assign, dtype=jnp.int32) // TOP_K)
    tile_start = jnp.arange(rows // tm, dtype=jnp.int32) * tm
    tile_expert = jnp.minimum(jnp.searchsorted(ends, tile_start, side="right"), N_EXPERTS - 1).astype(jnp.int32)
    tile_rows = jnp.clip((starts + counts)[tile_expert] - tile_start, 0, tm)
    tile_rows = jnp.where(tile_start < ends[-1], tile_rows, 0).astype(jnp.int32)
    last_valid = jnp.maximum(ends[-1] // tm - 1, 0)
    tile_expert = jnp.where(tile_rows > 0, tile_expert, tile_expert[last_valid])
    x_sorted = jnp.take(v, row_token, axis=0)
    y_sorted = _moe_experts(tile_expert, tile_rows, x_sorted, wg, wu, wd, j)
    dest2 = dest.reshape(t, TOP_K)
    y0 = jnp.take(y_sorted, dest2[:, 0], axis=0)
    y1 = jnp.take(y_sorted, dest2[:, 1], axis=0)
    return _combine(h, y0, y1, route, mod, rows_per_mod)


def _pad_cols(a, n):
    return jnp.pad(a, ((0, 0), (0, n - a.shape[1])))


def _prep_w_in(w):
    z, xbc, dtr, pool, cq, ckv, kr = jnp.split(w, np.cumsum(IN_SIZES)[:-1].tolist(), axis=1)
    parts = [z, xbc, pool, cq, ckv, _pad_cols(kr, LANES), _pad_cols(dtr[:, :SSD_HEADS], LANES),
             _pad_cols(dtr[:, SSD_HEADS:], LANES)]
    return jnp.concatenate(parts, axis=1).astype(BF16)


def _prep_w_uq(w):
    w = w.reshape(Q_LORA, MLA_HEADS, MLA_QK)
    w = jnp.pad(w, ((0, 0), (0, 0), (0, MLA_QKP - MLA_QK)))
    return w.reshape(Q_LORA, MLA_HEADS * MLA_QKP).astype(BF16)


def _prep_w_ukv(w):
    w = w.reshape(KV_LORA, MLA_HEADS, MLA_NOPE + MLA_V)
    k = w[:, :, :MLA_NOPE].reshape(KV_LORA, MLA_HEADS * MLA_NOPE)
    v = w[:, :, MLA_NOPE:].reshape(KV_LORA, MLA_HEADS * MLA_V)
    return jnp.concatenate([k, v], axis=1).astype(BF16)


def _rope_tables(seq):
    quarter = MLA_ROPE // 4
    inv = ROPE_BASE ** (-jnp.arange(quarter, dtype=F32) / quarter)
    pos = jnp.arange(seq, dtype=jnp.int32)
    pos_row = (pos // GRID_W).astype(F32)
    pos_col = (pos % GRID_W).astype(F32)
    ang_row = pos_row[:, None] * inv[None, :]
    ang_col = pos_col[:, None] * inv[None, :]
    ang = jnp.concatenate([ang_row, ang_row, ang_col, ang_col], axis=1)
    sign = jnp.tile(jnp.concatenate([-jnp.ones((quarter,), F32), jnp.ones((quarter,), F32)]), 2)
    cos = jnp.concatenate([jnp.cos(ang), jnp.ones((seq, LANES - MLA_ROPE), F32)], axis=1)
    sin = jnp.concatenate([jnp.sin(ang) * sign[None, :], jnp.zeros((seq, LANES - MLA_ROPE), F32)], axis=1)
    return cos, sin


def _head_expand():
    k = np.arange(LANES)[:, None]
    j = np.arange(SSD_W)[None, :]
    return jnp.asarray((k == j // SSD_HEADDIM).astype(np.float32), dtype=BF16)


def _split_hi_lo(w):
    hi = w.astype(BF16)
    lo = (w - hi.astype(F32)).astype(BF16)
    return hi, lo


def kernel(x, c, ctx, c_ctx, ada_w, ada_b, norm_mix_w, norm_ffn_w, w_in, ssd_conv_w, ssd_conv_b, ssd_dt_bias, ssd_a_log, ssd_d, ssd_norm_w, pool_w, pool_scale, mla_q_norm_w, mla_w_uq, mla_kv_norm_w, mla_w_ukv, mla_q_head_norm_w, mla_k_head_norm_w, w_out, ffn_w_gate, ffn_w_up, ffn_w_down, moe_w_router, moe_w_gate, moe_w_up, moe_w_down):
    bsz, seq, d = x.shape
    lc = ctx.shape[1]
    depth = ada_w.shape[0]
    h_l = x.reshape(bsz * seq, d)
    h_c = ctx.reshape(bsz * lc, d)
    t_l, t_c = h_l.shape[0], h_c.shape[0]

    pad_rows = (-(bsz + 1)) % SUBLANES
    cvec = jnp.concatenate([c, c_ctx[None, :], jnp.zeros((pad_rows, d), F32)], axis=0)
    rope_tabs = _rope_tables(seq)
    expand = _head_expand()
    zero_state = jnp.zeros((bsz, 2, SSD_GROUPS, SSD_STATE, SSD_GW), F32)

    for i in range(depth):
        last = i == depth - 1
        j = i // 2
        mod = _ada(cvec, ada_w[i], ada_b[i][None, :])
        mod_l = mod[:bsz].reshape(bsz, 6, d)
        mod_c = mod[bsz:bsz + 1].reshape(1, 6, d)

        w_in_i = _prep_w_in(w_in[i])
        nmw = norm_mix_w[i][None, :]
        z_l, xbc_l, pool_l, cq_l, ckv_l, kr_l, dt_l = _in_proj(h_l, mod_l, nmw, w_in_i, seq)
        z_c, xbc_c, pool_c, cq_c, ckv_c, kr_c, dt_c = _in_proj(h_c, mod_c, nmw, w_in_i, t_c)

        conv_b = ssd_conv_b[i][None, :]
        dtb = _pad_cols(ssd_dt_bias[i], LANES)[:, None, :]
        alog = _pad_cols(ssd_a_log[i], LANES)[:, None, :]
        dsk = jnp.repeat(ssd_d[i], SSD_HEADDIM)[None, :]
        snw = ssd_norm_w[i][None, :]
        ssd_args = (ssd_conv_w[i], conv_b, dtb, alog, dsk, snw, expand, bsz)
        ssd_c, st_c = _ssd(xbc_c, z_c, dt_c, zero_state, *ssd_args)
        ssd_l, _ = _ssd(xbc_l, z_l, dt_l, st_c, *ssd_args)

        pw = pool_w[i].astype(BF16)
        psc = pool_scale[i][None, :]
        pool_out_l = _pool(pool_l, pw, psc, bsz)

        qhn = _pad_cols(mla_q_head_norm_w[i][None, :], MLA_QKP)
        khn = _pad_cols(mla_k_head_norm_w[i][None, :], MLA_QKP)
        mla_args = (mla_q_norm_w[i][None, :], _prep_w_uq(mla_w_uq[i]), mla_kv_norm_w[i][None, :],
                    _prep_w_ukv(mla_w_ukv[i]), qhn, khn)
        q_l, k_l, v_l = _mla_proj(cq_l, ckv_l, kr_l, rope_tabs, *mla_args, with_q=True)
        if last:
            k_c, v_c = _mla_proj(cq_c, ckv_c, kr_c, None, *mla_args, with_q=False)
        else:
            q_c, k_c, v_c = _mla_proj(cq_c, ckv_c, kr_c, None, *mla_args, with_q=True)
        att_l = _attention(q_l, k_c, v_c, k_l, v_l, bsz)

        w_out_i = w_out[i].astype(BF16)
        nfw = norm_ffn_w[i][None, :]
        dense = i % 2 == 0
        router = None if dense else _split_hi_lo(_pad_cols(moe_w_router[j], LANES))
        res_l = _out_proj(ssd_l, pool_out_l, att_l, h_l, mod_l, nfw, w_out_i, seq, router)
        if dense:
            wg, wu, wd = ffn_w_gate, ffn_w_up, ffn_w_down
            h_l = _ffn(res_l[1], wg, wu, wd, j, res_l[0], mod_l, seq)
        else:
            wg, wu, wd = moe_w_gate, moe_w_up, moe_w_down
            h_l = _moe(res_l[1], res_l[2], res_l[0], mod_l, wg, wu, wd, j, seq)
        if not last:
            pool_out_c = _pool(pool_c, pw, psc, bsz)
            att_c = _attention(q_c, k_c, v_c, None, None, bsz)
            res_c = _out_proj(ssd_c, pool_out_c, att_c, h_c, mod_c, nfw, w_out_i, t_c, router)
            if dense:
                h_c = _ffn(res_c[1], wg, wu, wd, j, res_c[0], mod_c, t_c)
            else:
                h_c = _moe(res_c[1], res_c[2], res_c[0], mod_c, wg, wu, wd, j, t_c)
    return h_l.reshape(bsz, seq, d)
```

```python
import functools

import numpy as np
import jax
import jax.numpy as jnp
from jax import lax
from jax.experimental import pallas as pl
from jax.experimental.pallas import tpu as pltpu

F32 = jnp.float32
BF16 = jnp.bfloat16

D_MODEL = 2048
GRID_W = 64
SSD_HEADS = 12
SSD_HEADDIM = 64
SSD_W = SSD_HEADS * SSD_HEADDIM
SSD_GROUPS = 2
SSD_HPG = SSD_HEADS // SSD_GROUPS
SSD_STATE = 128
SSD_CONV = 4
SSD_CHUNK = 128
SSD_BC = SSD_GROUPS * SSD_STATE
SSD_CONV_CH = SSD_W + 2 * SSD_BC
SSD_GW = SSD_HPG * SSD_HEADDIM
POOL_WINDOWS = (2, 4, 8, 16)
POOL_GROUPS = 4
POOL_GC = 128
POOL_W = POOL_GROUPS * POOL_GC
POOL_HALO = 8
MLA_HEADS = 6
MLA_NOPE = 128
MLA_ROPE = 64
MLA_QK = MLA_NOPE + MLA_ROPE
MLA_QKP = 256
MLA_V = 128
MLA_W = MLA_HEADS * MLA_V
Q_LORA = 512
KV_LORA = 512
ROPE_BASE = 10000.0
LOG2_E = 1.4426950408889634
FFN_DIM = 5632
N_EXPERTS = 8
TOP_K = 2
NORM_EPS = 1e-6
IN_SIZES = (SSD_W, SSD_CONV_CH, 2 * SSD_HEADS, POOL_W, Q_LORA, KV_LORA, MLA_ROPE)
LANES = 128
SUBLANES = 8
IN_Z = 0
IN_XBC = IN_Z + SSD_W
IN_POOL = IN_XBC + SSD_CONV_CH
IN_CQ = IN_POOL + POOL_W
IN_CKV = IN_CQ + Q_LORA
IN_KR = IN_CKV + KV_LORA
IN_DT0 = IN_KR + LANES
IN_DT1 = IN_DT0 + LANES
IN_COLS = IN_DT1 + LANES
MIB = 1024 * 1024


def _cparams(sem, vmem_mib):
    return pltpu.CompilerParams(dimension_semantics=sem, vmem_limit_bytes=vmem_mib * MIB)


def _sigmoid(x):
    return 1.0 / (1.0 + jnp.exp(-x))


def _silu(x):
    return x * _sigmoid(x)


def _softplus(x):
    return jnp.maximum(x, 0.0) + jnp.log(1.0 + jnp.exp(-jnp.abs(x)))


def _rms(x, w):
    return x * lax.rsqrt(jnp.mean(x * x, axis=-1, keepdims=True) + NORM_EPS) * w


def _bdot(a, b):
    return jnp.dot(a, b, preferred_element_type=F32)


def _split_bf16(x, passes):
    parts = []
    r = x
    for _ in range(passes):
        p = r.astype(BF16)
        parts.append(p)
        r = r - p.astype(F32)
    return parts


def _ldot(m_bf16, x, passes):
    out = None
    for p in _split_bf16(x, passes):
        t = _bdot(m_bf16, p)
        out = t if out is None else out + t
    return out


def _rdot(x, m_bf16, passes):
    out = None
    for p in _split_bf16(x, passes):
        t = _bdot(p, m_bf16)
        out = t if out is None else out + t
    return out


def _ada_kernel(c_ref, w_ref, b_ref, o_ref):
    s = _silu(c_ref[...]).astype(BF16)
    o_ref[...] = _bdot(s, w_ref[0].astype(BF16)) + b_ref[...]


def _ada(cvec, w, b, layer):
    rows, d = cvec.shape
    n = w.shape[2]
    tn = 1024
    return pl.pallas_call(
        _ada_kernel,
        grid=(n // tn,),
        in_specs=[pl.BlockSpec((rows, d), lambda j: (0, 0)),
                  pl.BlockSpec((1, d, tn), lambda j: (layer, 0, j)),
                  pl.BlockSpec((1, tn), lambda j: (0, j))],
        out_specs=pl.BlockSpec((rows, tn), lambda j: (0, j)),
        out_shape=jax.ShapeDtypeStruct((rows, n), F32),
        compiler_params=_cparams(("arbitrary",), 40),
        name="ada",
    )(cvec, w, b)


IN_HALO = 16


def _in_proj_kernel(h_ref, hp_ref, hn_ref, mod_ref, nw_ref, w_ref, cw_ref, cb_ref, z_ref, xbc_ref, pool_ref,
                    cq_ref, ckv_ref, kr_ref, dt_ref, ue_scr, xe_scr, *, tiles_per_seq):
    tm = h_ref.shape[0]
    seq_tile = pl.program_id(0) % tiles_per_seq
    mod = mod_ref[0]

    def modulated(h):
        return (_rms(h, nw_ref[...]) * (1.0 + mod[1:2, :]) + mod[0:1, :]).astype(BF16)

    ue_scr[0:IN_HALO, :] = modulated(hp_ref[...])
    ue_scr[IN_HALO:IN_HALO + tm, :] = modulated(h_ref[...])
    ue_scr[IN_HALO + tm:, :] = modulated(hn_ref[...])
    xe = _bdot(ue_scr[...], w_ref[:, IN_XBC:IN_POOL])
    xe_scr[0:IN_HALO, :] = jnp.where(seq_tile == 0, 0.0, xe[0:IN_HALO, :])
    xe_scr[IN_HALO:IN_HALO + tm, :] = xe[IN_HALO:IN_HALO + tm, :]
    xe_scr[IN_HALO + tm:, :] = jnp.where(seq_tile == tiles_per_seq - 1, 0.0, xe[IN_HALO + tm:, :])

    cw = cw_ref[...]
    off = IN_HALO - SSD_CONV // 2
    acc = cb_ref[...]
    for k in range(SSD_CONV):
        acc = acc + cw[k:k + 1, :] * xe_scr[off + k:off + k + tm, :]
    xbc_ref[...] = _silu(acc)

    ub = ue_scr[IN_HALO:IN_HALO + tm, :]
    z_ref[...] = _bdot(ub, w_ref[:, IN_Z:IN_XBC])
    pool_ref[...] = _bdot(ub, w_ref[:, IN_POOL:IN_CQ])
    cq_ref[...] = _bdot(ub, w_ref[:, IN_CQ:IN_CKV])
    ckv_ref[...] = _bdot(ub, w_ref[:, IN_CKV:IN_KR])
    kr_ref[...] = _bdot(ub, w_ref[:, IN_KR:IN_DT0])
    dt_ref[0] = _bdot(ub, w_ref[:, IN_DT0:IN_DT1])
    dt_ref[1] = _bdot(ub, w_ref[:, IN_DT1:IN_COLS])


def _in_proj(h, mod, norm_w, w, conv_w, conv_b, rows_per_mod, seq):
    t, d = h.shape
    tm = 256
    tiles_per_mod = rows_per_mod // tm
    halo_per_tile = tm // IN_HALO
    widths = (SSD_W, SSD_CONV_CH, POOL_W, Q_LORA, KV_LORA, LANES)
    row = lambda i: (i, 0)
    const = lambda i: (0, 0)
    return pl.pallas_call(
        functools.partial(_in_proj_kernel, tiles_per_seq=seq // tm),
        grid=(t // tm,),
        in_specs=[pl.BlockSpec((tm, d), row),
                  pl.BlockSpec((IN_HALO, d), lambda i: (jnp.maximum(i * halo_per_tile - 1, 0), 0)),
                  pl.BlockSpec((IN_HALO, d), lambda i: (jnp.minimum((i + 1) * halo_per_tile, t // IN_HALO - 1), 0)),
                  pl.BlockSpec((1, 6, d), lambda i: (i // tiles_per_mod, 0, 0)),
                  pl.BlockSpec((1, d), const),
                  pl.BlockSpec((d, IN_COLS), const),
                  pl.BlockSpec((SSD_CONV, SSD_CONV_CH), const),
                  pl.BlockSpec((1, SSD_CONV_CH), const)],
        out_specs=[pl.BlockSpec((tm, wd), row) for wd in widths]
        + [pl.BlockSpec((2, tm, LANES), lambda i: (0, i, 0))],
        out_shape=[jax.ShapeDtypeStruct((t, wd), F32) for wd in widths]
        + [jax.ShapeDtypeStruct((2, t, LANES), F32)],
        scratch_shapes=[pltpu.VMEM((tm + 2 * IN_HALO, d), BF16),
                        pltpu.VMEM((tm + 2 * IN_HALO, SSD_CONV_CH), F32)],
        compiler_params=_cparams(("parallel",), 56),
        name="in_proj",
    )(h, h, h, mod, norm_w, w, conv_w, conv_b)


def _ssd_kernel(xbc_ref, z_ref, dt_ref, init_ref, dtb_ref, alog_ref, dsk_ref, nw_ref, e_ref,
                y_ref, fin_ref, yf_scr, st_scr, *, nc):
    q = SSD_CHUNK
    t = pl.program_id(1)
    d = t // nc
    cpos = t % nc
    c = jnp.where(d == 0, cpos, nc - 1 - cpos)
    r0 = pl.multiple_of(c * q, q)

    @pl.when(cpos == 0)
    def _():
        st_scr[...] = init_ref[0, 0]

    xbc = xbc_ref[...]
    xs = xbc[:, :SSD_W]

    lane = lax.broadcasted_iota(jnp.int32, (q, LANES), 1)
    row = lax.broadcasted_iota(jnp.int32, (q, LANES), 0)
    dt = jnp.where(lane < SSD_HEADS, _softplus(dt_ref[0] + dtb_ref[0]), 0.0)
    adt = dt * (-jnp.exp(alog_ref[0]))
    causal = (row - lane) * (1 - 2 * d) >= 0
    tri = jnp.where(causal, 1.0, 0.0).astype(BF16)
    cs = _ldot(tri, adt, 3)
    tot = jnp.sum(adt, axis=0, keepdims=True)
    e = e_ref[...]
    dtx = _rdot(dt, e, 2)
    ecsx = _rdot(jnp.exp(cs), e, 2)
    wx = _rdot(dt * jnp.exp(tot - cs), e, 2)
    etot = _rdot(jnp.broadcast_to(jnp.exp(tot), (SUBLANES, LANES)), e, 2)[0:1, :]
    xdt = xs * dtx
    xw = (xs * wx).astype(BF16)
    cs_t = cs.T
    low_half = lane < SSD_HEADDIM

    ys = []
    for g in range(SSD_GROUPS):
        bg = xbc[:, SSD_W + g * SSD_STATE:SSD_W + (g + 1) * SSD_STATE]
        cg = xbc[:, SSD_W + SSD_BC + g * SSD_STATE:SSD_W + SSD_BC + (g + 1) * SSD_STATE]
        cgb = cg.astype(BF16)
        cb = lax.dot_general(cgb, bg.astype(BF16), (((1,), (1,)), ((), ())),
                             preferred_element_type=F32)
        gs = slice(g * SSD_GW, (g + 1) * SSD_GW)
        st = st_scr[g]
        y_off = _bdot(cgb, st.astype(BF16)) * ecsx[:, gs]
        st_scr[g] = st * etot[:, gs] + _bdot(bg.T.astype(BF16), xw[:, gs])
        for pr in range(SSD_HPG // 2):
            xpair = xdt[:, g * SSD_GW + pr * LANES:g * SSD_GW + (pr + 1) * LANES]
            y_pair = y_off[:, pr * LANES:(pr + 1) * LANES]
            for k in range(2):
                h = g * SSD_HPG + pr * 2 + k
                dcs = cs[:, h:h + 1] - cs_t[h:h + 1, :]
                m = (cb * jnp.exp(jnp.where(causal, dcs, -1e30))).astype(BF16)
                xh = jnp.where(low_half if k == 0 else jnp.logical_not(low_half), xpair, 0.0)
                y_pair = y_pair + _bdot(m, xh.astype(BF16))
            ys.append(y_pair)
    y = jnp.concatenate(ys, axis=1)

    @pl.when(d == 0)
    def _():
        yf_scr[pl.ds(r0, q), :] = y

    @pl.when(d == 1)
    def _():
        g_ = (yf_scr[pl.ds(r0, q), :] + y + xs * dsk_ref[...]) * _silu(z_ref[...])
        y_ref[...] = _rms(g_, nw_ref[...]).astype(BF16)

    @pl.when(cpos == nc - 1)
    def _():
        fin_ref[0, 0] = st_scr[...]


def _ssd(xbc, z, dt, init, dt_bias, a_log, d_skip, norm_w, expand, bsz):
    t = xbc.shape[0]
    q = SSD_CHUNK
    nc = t // bsz // q

    def chunk(tt):
        cp = tt % nc
        return jnp.where(tt < nc, cp, nc - 1 - cp)

    def out_chunk(tt):
        return jnp.where(tt < nc, nc - 1, 2 * nc - 1 - tt)

    const2 = lambda b, tt: (0, 0)
    state_spec = pl.BlockSpec((1, 1, SSD_GROUPS, SSD_STATE, SSD_GW), lambda b, tt: (b, tt // nc, 0, 0, 0))
    state_shape = jax.ShapeDtypeStruct((bsz, 2, SSD_GROUPS, SSD_STATE, SSD_GW), F32)
    return pl.pallas_call(
        functools.partial(_ssd_kernel, nc=nc),
        grid=(bsz, 2 * nc),
        in_specs=[
            pl.BlockSpec((q, SSD_CONV_CH), lambda b, tt: (b * nc + chunk(tt), 0)),
            pl.BlockSpec((q, SSD_W), lambda b, tt: (b * nc + out_chunk(tt), 0)),
            pl.BlockSpec((1, q, LANES), lambda b, tt: (tt // nc, b * nc + chunk(tt), 0)),
            state_spec,
            pl.BlockSpec((1, 1, LANES), lambda b, tt: (tt // nc, 0, 0)),
            pl.BlockSpec((1, 1, LANES), lambda b, tt: (tt // nc, 0, 0)),
            pl.BlockSpec((1, SSD_W), const2),
            pl.BlockSpec((1, SSD_W), const2),
            pl.BlockSpec((LANES, SSD_W), const2),
        ],
        out_specs=[pl.BlockSpec((q, SSD_W), lambda b, tt: (b * nc + out_chunk(tt), 0)), state_spec],
        out_shape=[jax.ShapeDtypeStruct((t, SSD_W), BF16), state_shape],
        scratch_shapes=[pltpu.VMEM((nc * q, SSD_W), F32),
                        pltpu.VMEM((SSD_GROUPS, SSD_STATE, SSD_GW), F32)],
        compiler_params=_cparams(("parallel", "arbitrary"), 40),
        name="ssd",
    )(xbc, z, dt, init, dt_bias, a_log, d_skip, norm_w, expand)


def _pool_kernel(um_ref, up_ref, un_ref, pw_ref, ps_ref, o_ref, pe_scr, *, seq):
    tp = um_ref.shape[0]
    j = pl.program_id(1)
    pe_scr[0:POOL_HALO, :] = jnp.where(j == 0, 0.0, up_ref[...])
    pe_scr[POOL_HALO:POOL_HALO + tp, :] = um_ref[...]
    pe_scr[POOL_HALO + tp:, :] = jnp.where(j == pl.num_programs(1) - 1, 0.0, un_ref[...])
    pos = j * tp + lax.broadcasted_iota(jnp.int32, (tp, 1), 0)
    for g, w in enumerate(POOL_WINDOWS):
        cols = slice(g * POOL_GC, (g + 1) * POOL_GC)
        start = POOL_HALO - w // 2
        win = pe_scr[start:start + tp, cols]
        for k in range(1, w):
            win = win + pe_scr[start + k:start + k + tp, cols]
        lo = jnp.maximum(pos - w // 2, 0)
        hi = jnp.minimum(pos - w // 2 + w, seq)
        pooled = win / (hi - lo).astype(F32) - um_ref[:, cols]
        o_ref[:, cols] = (_bdot(pooled.astype(BF16), pw_ref[g]) * ps_ref[:, cols]).astype(BF16)


def _pool(u, pool_w, pool_scale, bsz):
    t = u.shape[0]
    seq = t // bsz
    tp = 256
    nt = seq // tp
    tiles8 = tp // SUBLANES
    return pl.pallas_call(
        functools.partial(_pool_kernel, seq=seq),
        grid=(bsz, nt),
        in_specs=[
            pl.BlockSpec((tp, POOL_W), lambda b, j: (b * nt + j, 0)),
            pl.BlockSpec((SUBLANES, POOL_W), lambda b, j: (jnp.maximum((b * nt + j) * tiles8 - 1, 0), 0)),
            pl.BlockSpec((SUBLANES, POOL_W),
                         lambda b, j: (jnp.minimum((b * nt + j + 1) * tiles8, t // SUBLANES - 1), 0)),
            pl.BlockSpec((POOL_GROUPS, POOL_GC, POOL_GC), lambda b, j: (0, 0, 0)),
            pl.BlockSpec((1, POOL_W), lambda b, j: (0, 0)),
        ],
        out_specs=pl.BlockSpec((tp, POOL_W), lambda b, j: (b * nt + j, 0)),
        out_shape=jax.ShapeDtypeStruct((t, POOL_W), BF16),
        scratch_shapes=[pltpu.VMEM((tp + 2 * POOL_HALO, POOL_W), F32)],
        compiler_params=_cparams(("parallel", "parallel"), 32),
        name="pool",
    )(u, u, u, pool_w, pool_scale)


def _rope(r, cos, sin_signed):
    lane = lax.broadcasted_iota(jnp.int32, r.shape, 1)
    quarter = MLA_ROPE // 4
    swapped = jnp.where(lane % (2 * quarter) < quarter,
                        pltpu.roll(r, LANES - quarter, 1), pltpu.roll(r, quarter, 1))
    return r * cos + swapped * sin_signed


def _mla_kernel(*refs, use_rope, with_q):
    if use_rope:
        cq_ref, ckv_ref, kr_ref, cos_ref, sin_ref = refs[:5]
        refs = refs[5:]
    else:
        cq_ref, ckv_ref, kr_ref = refs[:3]
        refs = refs[3:]
    qn_ref, wq_ref, kvn_ref, wkv_ref, qhn_ref, khn_ref = refs[:6]
    outs = refs[6:]
    if with_q:
        q_ref, k_ref, v_ref = outs
    else:
        k_ref, v_ref = outs
    rope = (lambda r: _rope(r, cos_ref[...], sin_ref[...])) if use_rope else (lambda r: r)
    inv_qk = 1.0 / MLA_QK

    if with_q:
        qf = _bdot(_rms(cq_ref[...], qn_ref[...]).astype(BF16), wq_ref[...])
        scale = MLA_QK ** -0.5 * LOG2_E
        for h in range(MLA_HEADS):
            qh = qf[:, h * MLA_QKP:(h + 1) * MLA_QKP]
            rs = lax.rsqrt(jnp.sum(qh * qh, axis=-1, keepdims=True) * inv_qk + NORM_EPS) * scale
            qh = qh * rs * qhn_ref[...]
            q_ref[:, h * MLA_QKP:h * MLA_QKP + MLA_NOPE] = qh[:, :MLA_NOPE].astype(BF16)
            q_ref[:, h * MLA_QKP + MLA_NOPE:(h + 1) * MLA_QKP] = rope(qh[:, MLA_NOPE:]).astype(BF16)

    kvf = _bdot(_rms(ckv_ref[...], kvn_ref[...]).astype(BF16), wkv_ref[...])
    kr = kr_ref[...]
    ss_kr = jnp.sum(kr * kr, axis=-1, keepdims=True)
    kr_w = rope(kr * khn_ref[:, MLA_NOPE:])
    for h in range(MLA_HEADS):
        kn = kvf[:, h * MLA_NOPE:(h + 1) * MLA_NOPE]
        rs = lax.rsqrt((jnp.sum(kn * kn, axis=-1, keepdims=True) + ss_kr) * inv_qk + NORM_EPS)
        k_ref[h * MLA_QKP:h * MLA_QKP + MLA_NOPE, :] = (kn * rs * khn_ref[:, :MLA_NOPE]).T.astype(BF16)
        k_ref[h * MLA_QKP + MLA_NOPE:(h + 1) * MLA_QKP, :] = (kr_w * rs).T.astype(BF16)
    v_ref[...] = kvf[:, MLA_HEADS * MLA_NOPE:].astype(BF16)


def _mla_proj(cq, ckv, kr, rope_tabs, qn, wq, kvn, wkv, qhn, khn, with_q):
    t = cq.shape[0]
    tm = 256
    use_rope = rope_tabs is not None
    row = lambda i: (i, 0)
    const = lambda i: (0, 0)
    ins = [cq, ckv, kr]
    in_specs = [pl.BlockSpec((tm, Q_LORA), row), pl.BlockSpec((tm, KV_LORA), row), pl.BlockSpec((tm, LANES), row)]
    if use_rope:
        cos, sin = rope_tabs
        nrt = cos.shape[0] // tm
        ins += [cos, sin]
        in_specs += [pl.BlockSpec((tm, LANES), lambda i: (i % nrt, 0))] * 2
    ins += [qn, wq, kvn, wkv, qhn, khn]
    in_specs += [pl.BlockSpec((1, Q_LORA), const), pl.BlockSpec((Q_LORA, MLA_HEADS * MLA_QKP), const),
                 pl.BlockSpec((1, KV_LORA), const), pl.BlockSpec((KV_LORA, MLA_HEADS * (MLA_NOPE + MLA_V)), const),
                 pl.BlockSpec((1, MLA_QKP), const), pl.BlockSpec((1, MLA_QKP), const)]
    out_specs = [pl.BlockSpec((MLA_HEADS * MLA_QKP, tm), lambda i: (0, i)), pl.BlockSpec((tm, MLA_W), row)]
    out_shape = [jax.ShapeDtypeStruct((MLA_HEADS * MLA_QKP, t), BF16), jax.ShapeDtypeStruct((t, MLA_W), BF16)]
    if with_q:
        out_specs = [pl.BlockSpec((tm, MLA_HEADS * MLA_QKP), row)] + out_specs
        out_shape = [jax.ShapeDtypeStruct((t, MLA_HEADS * MLA_QKP), BF16)] + out_shape
    return pl.pallas_call(
        functools.partial(_mla_kernel, use_rope=use_rope, with_q=with_q),
        grid=(t // tm,),
        in_specs=in_specs,
        out_specs=out_specs,
        out_shape=out_shape,
        compiler_params=_cparams(("parallel",), 40),
        name="mla_proj",
    )(*ins)


ATTN_CHAIN_ROWS = 256
ATTN_CHAINS = 2


def _attn_kernel(*refs, latent_keys, chains):
    if latent_keys:
        q_ref, kc_ref, vc_ref, kl_ref, vl_ref, o_ref = refs
    else:
        q_ref, kc_ref, vc_ref, o_ref = refs
    sub = ATTN_CHAIN_ROWS

    def body(it, carry):
        for ch in range(chains):
            rows = pl.ds(pl.multiple_of((it * chains + ch) * sub, sub), sub)
            q = q_ref[rows, :]
            sc = _bdot(q, kc_ref[...])
            m = jnp.max(sc, axis=-1, keepdims=True)
            if latent_keys:
                sl = _bdot(q, kl_ref[...])
                m = jnp.maximum(m, jnp.max(sl, axis=-1, keepdims=True))
            pc = jnp.exp2(sc - m)
            den = jnp.sum(pc, axis=-1, keepdims=True)
            o = _bdot(pc.astype(BF16), vc_ref[...])
            if latent_keys:
                pl_ = jnp.exp2(sl - m)
                den = den + jnp.sum(pl_, axis=-1, keepdims=True)
                o = o + _bdot(pl_.astype(BF16), vl_ref[...])
            o_ref[rows, :] = (o / den).astype(BF16)
        return carry

    lax.fori_loop(0, q_ref.shape[0] // (chains * sub), body, 0)


def _attention(q, kc, vc, kl, vl, bsz):
    tq_total = q.shape[0]
    lq = tq_total // bsz
    lc = kc.shape[1] // bsz
    chains = min(ATTN_CHAINS, lq // ATTN_CHAIN_ROWS)
    latent_keys = kl is not None
    ins = [q, kc, vc]
    in_specs = [pl.BlockSpec((lq, MLA_QKP), lambda b, h: (b, h)),
                pl.BlockSpec((MLA_QKP, lc), lambda b, h: (h, b)),
                pl.BlockSpec((lc, MLA_V), lambda b, h: (b, h))]
    if latent_keys:
        ll = kl.shape[1] // bsz
        ins += [kl, vl]
        in_specs += [pl.BlockSpec((MLA_QKP, ll), lambda b, h: (h, b)),
                     pl.BlockSpec((ll, MLA_V), lambda b, h: (b, h))]
    return pl.pallas_call(
        functools.partial(_attn_kernel, latent_keys=latent_keys, chains=chains),
        grid=(bsz, MLA_HEADS),
        in_specs=in_specs,
        out_specs=pl.BlockSpec((lq, MLA_V), lambda b, h: (b, h)),
        out_shape=jax.ShapeDtypeStruct((tq_total, MLA_W), BF16),
        compiler_params=_cparams(("parallel", "parallel"), 48),
        name="attention",
    )(*ins)


def _out_proj_kernel(*refs, route):
    ssd_ref, pool_ref, att_ref, h_ref, mod_ref, nw_ref, w_ref = refs[:7]
    refs = refs[7:]
    if route:
        wr_hi_ref, wr_lo_ref, ho_ref, v_ref, r_ref = refs
    else:
        ho_ref, v_ref = refs
    mod = mod_ref[0]
    for r0 in range(0, h_ref.shape[0], OUT_SUB):
        rows = slice(r0, r0 + OUT_SUB)
        mix = (_bdot(ssd_ref[rows, :], w_ref[0:SSD_W, :])
               + _bdot(pool_ref[rows, :], w_ref[SSD_W:SSD_W + POOL_W, :])
               + _bdot(att_ref[rows, :], w_ref[SSD_W + POOL_W:, :]))
        hn = h_ref[rows, :] + mod[2:3, :] * mix
        ho_ref[rows, :] = hn
        v = _rms(hn, nw_ref[...]) * (1.0 + mod[4:5, :]) + mod[3:4, :]
        vb = v.astype(BF16)
        v_ref[rows, :] = vb
        if route:
            v_lo = (v - vb.astype(F32)).astype(BF16)
            logits = _bdot(vb, wr_hi_ref[...]) + _bdot(v_lo, wr_hi_ref[...]) + _bdot(vb, wr_lo_ref[...])
            lane = lax.broadcasted_iota(jnp.int32, logits.shape, 1).astype(F32)
            neg = -jnp.inf
            logits = jnp.where(lane < N_EXPERTS, logits, neg)
            m1 = jnp.max(logits, axis=-1, keepdims=True)
            i1 = jnp.min(jnp.where(logits == m1, lane, float(LANES)), axis=-1, keepdims=True)
            rest = jnp.where(lane == i1, neg, logits)
            m2 = jnp.max(rest, axis=-1, keepdims=True)
            i2 = jnp.min(jnp.where(rest == m2, lane, float(LANES)), axis=-1, keepdims=True)
            e2 = jnp.exp(m2 - m1)
            g1 = 1.0 / (1.0 + e2)
            g2 = e2 / (1.0 + e2)
            r_ref[rows, :] = jnp.where(lane == 0, i1, jnp.where(lane == 1, i2, jnp.where(lane == 2, g1, jnp.where(lane == 3, g2, 0.0))))


OUT_TM = 512
OUT_SUB = 256


def _out_proj(ssd, pool, att, h, mod, norm_w, w, rows_per_mod, router=None):
    t, d = h.shape
    tm = OUT_TM
    tiles_per_mod = rows_per_mod // tm
    row = lambda i: (i, 0)
    const = lambda i: (0, 0)
    route = router is not None
    ins = [ssd, pool, att, h, mod, norm_w, w]
    in_specs = [pl.BlockSpec((tm, SSD_W), row), pl.BlockSpec((tm, POOL_W), row), pl.BlockSpec((tm, MLA_W), row),
                pl.BlockSpec((tm, d), row), pl.BlockSpec((1, 6, d), lambda i: (i // tiles_per_mod, 0, 0)),
                pl.BlockSpec((1, d), const), pl.BlockSpec((d, d), const)]
    out_specs = [pl.BlockSpec((tm, d), row), pl.BlockSpec((tm, d), row)]
    out_shape = [jax.ShapeDtypeStruct((t, d), F32), jax.ShapeDtypeStruct((t, d), BF16)]
    if route:
        ins += list(router)
        in_specs += [pl.BlockSpec((d, LANES), const)] * 2
        out_specs.append(pl.BlockSpec((tm, LANES), row))
        out_shape.append(jax.ShapeDtypeStruct((t, LANES), F32))
    return pl.pallas_call(
        functools.partial(_out_proj_kernel, route=route),
        grid=(t // tm,),
        in_specs=in_specs,
        out_specs=out_specs,
        out_shape=out_shape,
        compiler_params=_cparams(("parallel",), 48),
        name="out_proj",
    )(*ins)


FFN_TM = 1024
FFN_TF = 256
MOE_TF = 512
FFN_SUB = 512


FFN_CAST_CHUNK = 512


def _swiglu_tile(x_ref, wg_ref, wu_ref, wd_ref, o_ref, row_blocks):
    d = x_ref.shape[1]
    ck = FFN_CAST_CHUNK
    g = [None] * len(row_blocks)
    u = [None] * len(row_blocks)
    for k0 in range(0, d, ck):
        wg = wg_ref[k0:k0 + ck, :].astype(BF16)
        wu = wu_ref[k0:k0 + ck, :].astype(BF16)
        for s, rows in enumerate(row_blocks):
            x = x_ref[rows, k0:k0 + ck]
            tg = _bdot(x, wg)
            tu = _bdot(x, wu)
            g[s] = tg if g[s] is None else g[s] + tg
            u[s] = tu if u[s] is None else u[s] + tu
    acts = [(_silu(g[s]) * u[s]).astype(BF16) for s in range(len(row_blocks))]
    for n0 in range(0, d, ck):
        wd = wd_ref[:, n0:n0 + ck].astype(BF16)
        for s, rows in enumerate(row_blocks):
            o_ref[rows, n0:n0 + ck] += _bdot(acts[s], wd)


def _ffn_kernel(v_ref, wg_ref, wu_ref, wd_ref, h_ref, mod_ref, o_ref, *, sub):
    f = pl.program_id(1)

    @pl.when(f == 0)
    def _():
        o_ref[...] = jnp.zeros_like(o_ref)

    row_blocks = [slice(s * sub, (s + 1) * sub) for s in range(v_ref.shape[0] // sub)]
    _swiglu_tile(v_ref, wg_ref.at[0], wu_ref.at[0], wd_ref.at[0], o_ref, row_blocks)

    @pl.when(f == pl.num_programs(1) - 1)
    def _():
        o_ref[...] = h_ref[...] + mod_ref[0][5:6, :] * o_ref[...]


def _ffn(v, wg, wu, wd, j, h, mod, rows_per_mod):
    t, d = h.shape
    tm = min(FFN_TM, rows_per_mod)
    tf = FFN_TF
    tiles_per_mod = rows_per_mod // tm
    nf = wg.shape[2] // tf
    return pl.pallas_call(
        functools.partial(_ffn_kernel, sub=min(FFN_SUB, tm)),
        grid=(t // tm, nf),
        in_specs=[pl.BlockSpec((tm, d), lambda i, f: (i, 0)),
                  pl.BlockSpec((1, d, tf), lambda i, f: (j, 0, f)),
                  pl.BlockSpec((1, d, tf), lambda i, f: (j, 0, f)),
                  pl.BlockSpec((1, tf, d), lambda i, f: (j, f, 0)),
                  pl.BlockSpec((tm, d), lambda i, f: (i, 0), pipeline_mode=pl.Buffered(1)),
                  pl.BlockSpec((1, 6, d), lambda i, f: (i // tiles_per_mod, 0, 0))],
        out_specs=pl.BlockSpec((tm, d), lambda i, f: (i, 0)),
        out_shape=jax.ShapeDtypeStruct((t, d), F32),
        compiler_params=_cparams(("parallel", "arbitrary"), 56),
        name="ffn",
    )(v, wg, wu, wd, h, mod)


def _moe_kernel(te_ref, tn_ref, x_ref, wg_ref, wu_ref, wd_ref, o_ref, acc_scr, *, sub):
    i = pl.program_id(0)
    f = pl.program_id(1)
    n_rows = tn_ref[i]

    @pl.when(n_rows > 0)
    def _():
        @pl.when(f == 0)
        def _():
            acc_scr[...] = jnp.zeros_like(acc_scr)

        refs = (x_ref, wg_ref.at[0, 0], wu_ref.at[0, 0], wd_ref.at[0, 0], acc_scr)
        n_blocks = x_ref.shape[0] // sub
        for live in range(1, n_blocks + 1):
            lo = (live - 1) * sub
            cond = (n_rows > lo) if live == n_blocks else jnp.logical_and(n_rows > lo, n_rows <= lo + sub)

            @pl.when(cond)
            def _():
                _swiglu_tile(*refs, [slice(s * sub, (s + 1) * sub) for s in range(live)])

        @pl.when(f == pl.num_programs(1) - 1)
        def _():
            o_ref[...] = acc_scr[...].astype(BF16)


def _moe_experts(tile_expert, tile_rows, x_sorted, wg, wu, wd, j):
    r, d = x_sorted.shape
    tm, tf = FFN_TM, MOE_TF
    nf = wg.shape[3] // tf

    def fidx(i, f, tn):
        return jnp.where(tn[i] > 0, f, nf - 1)

    grid_spec = pltpu.PrefetchScalarGridSpec(
        num_scalar_prefetch=2,
        grid=(r // tm, nf),
        in_specs=[pl.BlockSpec((tm, d), lambda i, f, te, tn: (i, 0)),
                  pl.BlockSpec((1, 1, d, tf), lambda i, f, te, tn: (j, te[i], 0, fidx(i, f, tn))),
                  pl.BlockSpec((1, 1, d, tf), lambda i, f, te, tn: (j, te[i], 0, fidx(i, f, tn))),
                  pl.BlockSpec((1, 1, tf, d), lambda i, f, te, tn: (j, te[i], fidx(i, f, tn), 0))],
        out_specs=pl.BlockSpec((tm, d), lambda i, f, te, tn: (i, 0)),
        scratch_shapes=[pltpu.VMEM((tm, d), F32)],
    )
    return pl.pallas_call(
        functools.partial(_moe_kernel, sub=FFN_SUB),
        grid_spec=grid_spec,
        out_shape=jax.ShapeDtypeStruct((r, d), BF16),
        compiler_params=_cparams(("arbitrary", "arbitrary"), 58),
        name="moe_experts",
    )(tile_expert, tile_rows, x_sorted, wg, wu, wd)


def _combine_kernel(h_ref, y0_ref, y1_ref, r_ref, mod_ref, o_ref):
    r = r_ref[...]
    mixed = r[:, 2:3] * y0_ref[...].astype(F32) + r[:, 3:4] * y1_ref[...].astype(F32)
    o_ref[...] = h_ref[...] + mod_ref[0][5:6, :] * mixed


def _combine(h, y0, y1, route, mod, rows_per_mod):
    t, d = h.shape
    tm = 512
    tiles_per_mod = rows_per_mod // tm
    row = lambda i: (i, 0)
    return pl.pallas_call(
        _combine_kernel,
        grid=(t // tm,),
        in_specs=[pl.BlockSpec((tm, d), row), pl.BlockSpec((tm, d), row), pl.BlockSpec((tm, d), row),
                  pl.BlockSpec((tm, LANES), row),
                  pl.BlockSpec((1, 6, d), lambda i: (i // tiles_per_mod, 0, 0))],
        out_specs=pl.BlockSpec((tm, d), row),
        out_shape=jax.ShapeDtypeStruct((t, d), F32),
        compiler_params=_cparams(("parallel",), 48),
        name="moe_combine",
    )(h, y0, y1, route, mod)


def _moe(v, route, h, mod, wg, wu, wd, j, rows_per_mod):
    t = v.shape[0]
    tm = FFN_TM
    n_assign = t * TOP_K
    rows = n_assign + N_EXPERTS * tm
    expert = route[:, :TOP_K].astype(jnp.int32).reshape(-1)
    onehot = (expert[:, None] == jnp.arange(N_EXPERTS, dtype=jnp.int32)[None, :]).astype(jnp.int32)
    csum = jnp.cumsum(onehot, axis=0)
    rank = jnp.sum(csum * onehot, axis=1) - 1
    counts = csum[-1]
    padded = ((counts + tm - 1) // tm) * tm
    ends = jnp.cumsum(padded)
    starts = ends - padded
    dest = starts[expert] + rank
    row_token = (jnp.arange(rows, dtype=jnp.int32) % t).at[dest].set(jnp.arange(n_assign, dtype=jnp.int32) // TOP_K)
    tile_start = jnp.arange(rows // tm, dtype=jnp.int32) * tm
    tile_expert = jnp.minimum(jnp.searchsorted(ends, tile_start, side="right"), N_EXPERTS - 1).astype(jnp.int32)
    tile_rows = jnp.clip((starts + counts)[tile_expert] - tile_start, 0, tm)
    tile_rows = jnp.where(tile_start < ends[-1], tile_rows, 0).astype(jnp.int32)
    last_valid = jnp.maximum(ends[-1] // tm - 1, 0)
    tile_expert = jnp.where(tile_rows > 0, tile_expert, tile_expert[last_valid])
    x_sorted = v.at[row_token].get(mode="promise_in_bounds")
    y_sorted = _moe_experts(tile_expert, tile_rows, x_sorted, wg, wu, wd, j)
    dest2 = dest.reshape(t, TOP_K)
    y0 = y_sorted.at[dest2[:, 0]].get(mode="promise_in_bounds")
    y1 = y_sorted.at[dest2[:, 1]].get(mode="promise_in_bounds")
    return _combine(h, y0, y1, route, mod, rows_per_mod)


def _pad_cols(a, n):
    return jnp.pad(a, ((0, 0), (0, n - a.shape[1])))


def _prep_w_in(w):
    z, xbc, dtr, pool, cq, ckv, kr = jnp.split(w, np.cumsum(IN_SIZES)[:-1].tolist(), axis=1)
    parts = [z, xbc, pool, cq, ckv, _pad_cols(kr, LANES), _pad_cols(dtr[:, :SSD_HEADS], LANES),
             _pad_cols(dtr[:, SSD_HEADS:], LANES)]
    return jnp.concatenate(parts, axis=1).astype(BF16)


def _prep_w_uq(w):
    w = w.reshape(Q_LORA, MLA_HEADS, MLA_QK)
    w = jnp.pad(w, ((0, 0), (0, 0), (0, MLA_QKP - MLA_QK)))
    return w.reshape(Q_LORA, MLA_HEADS * MLA_QKP).astype(BF16)


def _prep_w_ukv(w):
    w = w.reshape(KV_LORA, MLA_HEADS, MLA_NOPE + MLA_V)
    k = w[:, :, :MLA_NOPE].reshape(KV_LORA, MLA_HEADS * MLA_NOPE)
    v = w[:, :, MLA_NOPE:].reshape(KV_LORA, MLA_HEADS * MLA_V)
    return jnp.concatenate([k, v], axis=1).astype(BF16)


def _rope_tables(seq):
    quarter = MLA_ROPE // 4
    inv = ROPE_BASE ** (-jnp.arange(quarter, dtype=F32) / quarter)
    pos = jnp.arange(seq, dtype=jnp.int32)
    pos_row = (pos // GRID_W).astype(F32)
    pos_col = (pos % GRID_W).astype(F32)
    ang_row = pos_row[:, None] * inv[None, :]
    ang_col = pos_col[:, None] * inv[None, :]
    ang = jnp.concatenate([ang_row, ang_row, ang_col, ang_col], axis=1)
    sign = jnp.tile(jnp.concatenate([-jnp.ones((quarter,), F32), jnp.ones((quarter,), F32)]), 2)
    cos = jnp.concatenate([jnp.cos(ang), jnp.ones((seq, LANES - MLA_ROPE), F32)], axis=1)
    sin = jnp.concatenate([jnp.sin(ang) * sign[None, :], jnp.zeros((seq, LANES - MLA_ROPE), F32)], axis=1)
    return cos, sin


def _head_expand():
    k = np.arange(LANES)[:, None]
    j = np.arange(SSD_W)[None, :]
    return jnp.asarray((k == j // SSD_HEADDIM).astype(np.float32), dtype=BF16)


def _split_hi_lo(w):
    hi = w.astype(BF16)
    lo = (w - hi.astype(F32)).astype(BF16)
    return hi, lo


def kernel(x, c, ctx, c_ctx, ada_w, ada_b, norm_mix_w, norm_ffn_w, w_in, ssd_conv_w, ssd_conv_b, ssd_dt_bias, ssd_a_log, ssd_d, ssd_norm_w, pool_w, pool_scale, mla_q_norm_w, mla_w_uq, mla_kv_norm_w, mla_w_ukv, mla_q_head_norm_w, mla_k_head_norm_w, w_out, ffn_w_gate, ffn_w_up, ffn_w_down, moe_w_router, moe_w_gate, moe_w_up, moe_w_down):
    bsz, seq, d = x.shape
    lc = ctx.shape[1]
    depth = ada_w.shape[0]
    h_l = x.reshape(bsz * seq, d)
    h_c = ctx.reshape(bsz * lc, d)
    t_l, t_c = h_l.shape[0], h_c.shape[0]

    pad_rows = (-(bsz + 1)) % SUBLANES
    cvec = jnp.concatenate([c, c_ctx[None, :], jnp.zeros((pad_rows, d), F32)], axis=0)
    rope_tabs = _rope_tables(seq)
    expand = _head_expand()
    zero_state = jnp.zeros((bsz, 2, SSD_GROUPS, SSD_STATE, SSD_GW), F32)

    for i in range(depth):
        last = i == depth - 1
        j = i // 2
        mod = _ada(cvec, ada_w, ada_b[i][None, :], i)
        mod_l = mod[:bsz].reshape(bsz, 6, d)
        mod_c = mod[bsz:bsz + 1].reshape(1, 6, d)

        w_in_i = _prep_w_in(w_in[i])
        nmw = norm_mix_w[i][None, :]
        conv_args = (ssd_conv_w[i], ssd_conv_b[i][None, :])
        z_l, xbc_l, pool_l, cq_l, ckv_l, kr_l, dt_l = _in_proj(h_l, mod_l, nmw, w_in_i, *conv_args, seq, seq)
        z_c, xbc_c, pool_c, cq_c, ckv_c, kr_c, dt_c = _in_proj(h_c, mod_c, nmw, w_in_i, *conv_args, t_c, lc)

        dtb = _pad_cols(ssd_dt_bias[i], LANES)[:, None, :]
        alog = _pad_cols(ssd_a_log[i], LANES)[:, None, :]
        dsk = jnp.repeat(ssd_d[i], SSD_HEADDIM)[None, :]
        snw = ssd_norm_w[i][None, :]
        ssd_args = (dtb, alog, dsk, snw, expand, bsz)
        ssd_c, st_c = _ssd(xbc_c, z_c, dt_c, zero_state, *ssd_args)
        ssd_l, _ = _ssd(xbc_l, z_l, dt_l, st_c, *ssd_args)

        pw = pool_w[i].astype(BF16)
        psc = pool_scale[i][None, :]
        pool_out_l = _pool(pool_l, pw, psc, bsz)

        qhn = _pad_cols(mla_q_head_norm_w[i][None, :], MLA_QKP)
        khn = _pad_cols(mla_k_head_norm_w[i][None, :], MLA_QKP)
        mla_args = (mla_q_norm_w[i][None, :], _prep_w_uq(mla_w_uq[i]), mla_kv_norm_w[i][None, :],
                    _prep_w_ukv(mla_w_ukv[i]), qhn, khn)
        q_l, k_l, v_l = _mla_proj(cq_l, ckv_l, kr_l, rope_tabs, *mla_args, with_q=True)
        if last:
            k_c, v_c = _mla_proj(cq_c, ckv_c, kr_c, None, *mla_args, with_q=False)
        else:
            q_c, k_c, v_c = _mla_proj(cq_c, ckv_c, kr_c, None, *mla_args, with_q=True)
        att_l = _attention(q_l, k_c, v_c, k_l, v_l, bsz)

        w_out_i = w_out[i].astype(BF16)
        nfw = norm_ffn_w[i][None, :]
        dense = i % 2 == 0
        router = None if dense else _split_hi_lo(_pad_cols(moe_w_router[j], LANES))
        res_l = _out_proj(ssd_l, pool_out_l, att_l, h_l, mod_l, nfw, w_out_i, seq, router)
        if dense:
            wg, wu, wd = ffn_w_gate, ffn_w_up, ffn_w_down
            h_l = _ffn(res_l[1], wg, wu, wd, j, res_l[0], mod_l, seq)
        else:
            wg, wu, wd = moe_w_gate, moe_w_up, moe_w_down
            h_l = _moe(res_l[1], res_l[2], res_l[0], mod_l, wg, wu, wd, j, seq)
        if not last:
            pool_out_c = _pool(pool_c, pw, psc, bsz)
            att_c = _attention(q_c, k_c, v_c, None, None, bsz)
            res_c = _out_proj(ssd_c, pool_out_c, att_c, h_c, mod_c, nfw, w_out_i, t_c, router)
            if dense:
                h_c = _ffn(res_c[1], wg, wu, wd, j, res_c[0], mod_c, t_c)
            else:
                h_c = _moe(res_c[1], res_c[2], res_c[0], mod_c, wg, wu, wd, j, t_c)
    return h_l.reshape(bsz, seq, d)
```

```python
import functools

import numpy as np
import jax
import jax.numpy as jnp
from jax import lax
from jax.experimental import pallas as pl
from jax.experimental.pallas import tpu as pltpu

F32 = jnp.float32
BF16 = jnp.bfloat16

D_MODEL = 2048
GRID_W = 64
SSD_HEADS = 12
SSD_HEADDIM = 64
SSD_W = SSD_HEADS * SSD_HEADDIM
SSD_GROUPS = 2
SSD_HPG = SSD_HEADS // SSD_GROUPS
SSD_STATE = 128
SSD_CONV = 4
SSD_CHUNK = 128
SSD_BC = SSD_GROUPS * SSD_STATE
SSD_CONV_CH = SSD_W + 2 * SSD_BC
SSD_GW = SSD_HPG * SSD_HEADDIM
POOL_WINDOWS = (2, 4, 8, 16)
POOL_GROUPS = 4
POOL_GC = 128
POOL_W = POOL_GROUPS * POOL_GC
POOL_HALO = 8
MLA_HEADS = 6
MLA_NOPE = 128
MLA_ROPE = 64
MLA_QK = MLA_NOPE + MLA_ROPE
MLA_QKP = 256
MLA_V = 128
MLA_W = MLA_HEADS * MLA_V
Q_LORA = 512
KV_LORA = 512
ROPE_BASE = 10000.0
LOG2_E = 1.4426950408889634
FFN_DIM = 5632
N_EXPERTS = 8
TOP_K = 2
NORM_EPS = 1e-6
IN_SIZES = (SSD_W, SSD_CONV_CH, 2 * SSD_HEADS, POOL_W, Q_LORA, KV_LORA, MLA_ROPE)
LANES = 128
SUBLANES = 8
IN_Z = 0
IN_XBC = IN_Z + SSD_W
IN_POOL = IN_XBC + SSD_CONV_CH
IN_CQ = IN_POOL + POOL_W
IN_CKV = IN_CQ + Q_LORA
IN_KR = IN_CKV + KV_LORA
IN_DT0 = IN_KR + LANES
IN_DT1 = IN_DT0 + LANES
IN_COLS = IN_DT1 + LANES
MIB = 1024 * 1024


def _cparams(sem, vmem_mib):
    return pltpu.CompilerParams(dimension_semantics=sem, vmem_limit_bytes=vmem_mib * MIB)


def _sigmoid(x):
    return 1.0 / (1.0 + jnp.exp(-x))


def _silu(x):
    return x * _sigmoid(x)


def _softplus(x):
    return jnp.maximum(x, 0.0) + jnp.log(1.0 + jnp.exp(-jnp.abs(x)))


def _rms(x, w):
    return x * lax.rsqrt(jnp.mean(x * x, axis=-1, keepdims=True) + NORM_EPS) * w


def _bdot(a, b):
    return jnp.dot(a, b, preferred_element_type=F32)


def _split_bf16(x, passes):
    parts = []
    r = x
    for _ in range(passes):
        p = r.astype(BF16)
        parts.append(p)
        r = r - p.astype(F32)
    return parts


def _ldot(m_bf16, x, passes):
    out = None
    for p in _split_bf16(x, passes):
        t = _bdot(m_bf16, p)
        out = t if out is None else out + t
    return out


def _rdot(x, m_bf16, passes):
    out = None
    for p in _split_bf16(x, passes):
        t = _bdot(p, m_bf16)
        out = t if out is None else out + t
    return out


def _ada_kernel(c_ref, w_ref, b_ref, o_ref):
    s = _silu(c_ref[...]).astype(BF16)
    o_ref[...] = _bdot(s, w_ref[0].astype(BF16)) + b_ref[...]


def _ada(cvec, w, b, layer):
    rows, d = cvec.shape
    n = w.shape[2]
    tn = 1024
    return pl.pallas_call(
        _ada_kernel,
        grid=(n // tn,),
        in_specs=[pl.BlockSpec((rows, d), lambda j: (0, 0)),
                  pl.BlockSpec((1, d, tn), lambda j: (layer, 0, j)),
                  pl.BlockSpec((1, tn), lambda j: (0, j))],
        out_specs=pl.BlockSpec((rows, tn), lambda j: (0, j)),
        out_shape=jax.ShapeDtypeStruct((rows, n), F32),
        compiler_params=_cparams(("arbitrary",), 40),
        name="ada",
    )(cvec, w, b)


IN_HALO = 16


def _in_proj_kernel(h_ref, hp_ref, hn_ref, mod_ref, nw_ref, w_ref, cw_ref, cb_ref, z_ref, xbc_ref, pool_ref,
                    cq_ref, ckv_ref, kr_ref, dt_ref, ue_scr, xe_scr, *, tiles_per_seq):
    tm = h_ref.shape[0]
    seq_tile = pl.program_id(0) % tiles_per_seq
    mod = mod_ref[0]

    def modulated(h):
        return (_rms(h, nw_ref[...]) * (1.0 + mod[1:2, :]) + mod[0:1, :]).astype(BF16)

    ue_scr[0:IN_HALO, :] = modulated(hp_ref[...])
    ue_scr[IN_HALO:IN_HALO + tm, :] = modulated(h_ref[...])
    ue_scr[IN_HALO + tm:, :] = modulated(hn_ref[...])
    xe = _bdot(ue_scr[...], w_ref[:, IN_XBC:IN_POOL])
    xe_scr[0:IN_HALO, :] = jnp.where(seq_tile == 0, 0.0, xe[0:IN_HALO, :])
    xe_scr[IN_HALO:IN_HALO + tm, :] = xe[IN_HALO:IN_HALO + tm, :]
    xe_scr[IN_HALO + tm:, :] = jnp.where(seq_tile == tiles_per_seq - 1, 0.0, xe[IN_HALO + tm:, :])

    cw = cw_ref[...]
    off = IN_HALO - SSD_CONV // 2
    acc = cb_ref[...]
    for k in range(SSD_CONV):
        acc = acc + cw[k:k + 1, :] * xe_scr[off + k:off + k + tm, :]
    xbc_ref[...] = _silu(acc)

    ub = ue_scr[IN_HALO:IN_HALO + tm, :]
    z_ref[...] = _bdot(ub, w_ref[:, IN_Z:IN_XBC])
    pool_ref[...] = _bdot(ub, w_ref[:, IN_POOL:IN_CQ])
    cq_ref[...] = _bdot(ub, w_ref[:, IN_CQ:IN_CKV])
    ckv_ref[...] = _bdot(ub, w_ref[:, IN_CKV:IN_KR])
    kr_ref[...] = _bdot(ub, w_ref[:, IN_KR:IN_DT0])
    dt_ref[0] = _bdot(ub, w_ref[:, IN_DT0:IN_DT1])
    dt_ref[1] = _bdot(ub, w_ref[:, IN_DT1:IN_COLS])


def _in_proj(h, mod, norm_w, w, conv_w, conv_b, rows_per_mod, seq):
    t, d = h.shape
    tm = 256
    tiles_per_mod = rows_per_mod // tm
    halo_per_tile = tm // IN_HALO
    widths = (SSD_W, SSD_CONV_CH, POOL_W, Q_LORA, KV_LORA, LANES)
    row = lambda i: (i, 0)
    const = lambda i: (0, 0)
    return pl.pallas_call(
        functools.partial(_in_proj_kernel, tiles_per_seq=seq // tm),
        grid=(t // tm,),
        in_specs=[pl.BlockSpec((tm, d), row),
                  pl.BlockSpec((IN_HALO, d), lambda i: (jnp.maximum(i * halo_per_tile - 1, 0), 0)),
                  pl.BlockSpec((IN_HALO, d), lambda i: (jnp.minimum((i + 1) * halo_per_tile, t // IN_HALO - 1), 0)),
                  pl.BlockSpec((1, 6, d), lambda i: (i // tiles_per_mod, 0, 0)),
                  pl.BlockSpec((1, d), const),
                  pl.BlockSpec((d, IN_COLS), const),
                  pl.BlockSpec((SSD_CONV, SSD_CONV_CH), const),
                  pl.BlockSpec((1, SSD_CONV_CH), const)],
        out_specs=[pl.BlockSpec((tm, wd), row) for wd in widths]
        + [pl.BlockSpec((2, tm, LANES), lambda i: (0, i, 0))],
        out_shape=[jax.ShapeDtypeStruct((t, wd), F32) for wd in widths]
        + [jax.ShapeDtypeStruct((2, t, LANES), F32)],
        scratch_shapes=[pltpu.VMEM((tm + 2 * IN_HALO, d), BF16),
                        pltpu.VMEM((tm + 2 * IN_HALO, SSD_CONV_CH), F32)],
        compiler_params=_cparams(("parallel",), 56),
        name="in_proj",
    )(h, h, h, mod, norm_w, w, conv_w, conv_b)


def _ssd_kernel(xbc_ref, z_ref, dt_ref, init_ref, dtb_ref, alog_ref, dsk_ref, nw_ref, e_ref,
                y_ref, fin_ref, yf_scr, st_scr, *, nc):
    q = SSD_CHUNK
    t = pl.program_id(1)
    d = t // nc
    cpos = t % nc
    c = jnp.where(d == 0, cpos, nc - 1 - cpos)
    r0 = pl.multiple_of(c * q, q)

    @pl.when(cpos == 0)
    def _():
        st_scr[...] = init_ref[0, 0]

    xbc = xbc_ref[...]
    xs = xbc[:, :SSD_W]

    lane = lax.broadcasted_iota(jnp.int32, (q, LANES), 1)
    row = lax.broadcasted_iota(jnp.int32, (q, LANES), 0)
    dt = jnp.where(lane < SSD_HEADS, _softplus(dt_ref[0] + dtb_ref[0]), 0.0)
    adt = dt * (-jnp.exp(alog_ref[0]))
    causal = (row - lane) * (1 - 2 * d) >= 0
    tri = jnp.where(causal, 1.0, 0.0).astype(BF16)
    cs = _ldot(tri, adt, 3)
    tot = jnp.sum(adt, axis=0, keepdims=True)
    e = e_ref[...]
    dtx = _rdot(dt, e, 2)
    ecsx = _rdot(jnp.exp(cs), e, 2)
    wx = _rdot(dt * jnp.exp(tot - cs), e, 2)
    etot = _rdot(jnp.broadcast_to(jnp.exp(tot), (SUBLANES, LANES)), e, 2)[0:1, :]
    xdt = xs * dtx
    xw = (xs * wx).astype(BF16)
    cs_t = cs.T
    low_half = lane < SSD_HEADDIM

    ys = []
    for g in range(SSD_GROUPS):
        bg = xbc[:, SSD_W + g * SSD_STATE:SSD_W + (g + 1) * SSD_STATE]
        cg = xbc[:, SSD_W + SSD_BC + g * SSD_STATE:SSD_W + SSD_BC + (g + 1) * SSD_STATE]
        cgb = cg.astype(BF16)
        cb = lax.dot_general(cgb, bg.astype(BF16), (((1,), (1,)), ((), ())),
                             preferred_element_type=F32)
        gs = slice(g * SSD_GW, (g + 1) * SSD_GW)
        st = st_scr[g]
        y_off = _bdot(cgb, st.astype(BF16)) * ecsx[:, gs]
        st_scr[g] = st * etot[:, gs] + _bdot(bg.T.astype(BF16), xw[:, gs])
        for pr in range(SSD_HPG // 2):
            xpair = xdt[:, g * SSD_GW + pr * LANES:g * SSD_GW + (pr + 1) * LANES]
            y_pair = y_off[:, pr * LANES:(pr + 1) * LANES]
            for k in range(2):
                h = g * SSD_HPG + pr * 2 + k
                dcs = cs[:, h:h + 1] - cs_t[h:h + 1, :]
                m = (cb * jnp.exp(jnp.where(causal, dcs, -1e30))).astype(BF16)
                xh = jnp.where(low_half if k == 0 else jnp.logical_not(low_half), xpair, 0.0)
                y_pair = y_pair + _bdot(m, xh.astype(BF16))
            ys.append(y_pair)
    y = jnp.concatenate(ys, axis=1)

    @pl.when(d == 0)
    def _():
        yf_scr[pl.ds(r0, q), :] = y

    @pl.when(d == 1)
    def _():
        g_ = (yf_scr[pl.ds(r0, q), :] + y + xs * dsk_ref[...]) * _silu(z_ref[...])
        y_ref[...] = _rms(g_, nw_ref[...]).astype(BF16)

    @pl.when(cpos == nc - 1)
    def _():
        fin_ref[0, 0] = st_scr[...]


def _ssd(xbc, z, dt, init, dt_bias, a_log, d_skip, norm_w, expand, bsz):
    t = xbc.shape[0]
    q = SSD_CHUNK
    nc = t // bsz // q

    def chunk(tt):
        cp = tt % nc
        return jnp.where(tt < nc, cp, nc - 1 - cp)

    def out_chunk(tt):
        return jnp.where(tt < nc, nc - 1, 2 * nc - 1 - tt)

    const2 = lambda b, tt: (0, 0)
    state_spec = pl.BlockSpec((1, 1, SSD_GROUPS, SSD_STATE, SSD_GW), lambda b, tt: (b, tt // nc, 0, 0, 0))
    state_shape = jax.ShapeDtypeStruct((bsz, 2, SSD_GROUPS, SSD_STATE, SSD_GW), F32)
    return pl.pallas_call(
        functools.partial(_ssd_kernel, nc=nc),
        grid=(bsz, 2 * nc),
        in_specs=[
            pl.BlockSpec((q, SSD_CONV_CH), lambda b, tt: (b * nc + chunk(tt), 0)),
            pl.BlockSpec((q, SSD_W), lambda b, tt: (b * nc + out_chunk(tt), 0)),
            pl.BlockSpec((1, q, LANES), lambda b, tt: (tt // nc, b * nc + chunk(tt), 0)),
            state_spec,
            pl.BlockSpec((1, 1, LANES), lambda b, tt: (tt // nc, 0, 0)),
            pl.BlockSpec((1, 1, LANES), lambda b, tt: (tt // nc, 0, 0)),
            pl.BlockSpec((1, SSD_W), const2),
            pl.BlockSpec((1, SSD_W), const2),
            pl.BlockSpec((LANES, SSD_W), const2),
        ],
        out_specs=[pl.BlockSpec((q, SSD_W), lambda b, tt: (b * nc + out_chunk(tt), 0)), state_spec],
        out_shape=[jax.ShapeDtypeStruct((t, SSD_W), BF16), state_shape],
        scratch_shapes=[pltpu.VMEM((nc * q, SSD_W), F32),
                        pltpu.VMEM((SSD_GROUPS, SSD_STATE, SSD_GW), F32)],
        compiler_params=_cparams(("parallel", "arbitrary"), 40),
        name="ssd",
    )(xbc, z, dt, init, dt_bias, a_log, d_skip, norm_w, expand)


def _pool_kernel(um_ref, up_ref, un_ref, pw_ref, ps_ref, o_ref, pe_scr, *, seq):
    tp = um_ref.shape[0]
    j = pl.program_id(1)
    pe_scr[0:POOL_HALO, :] = jnp.where(j == 0, 0.0, up_ref[...])
    pe_scr[POOL_HALO:POOL_HALO + tp, :] = um_ref[...]
    pe_scr[POOL_HALO + tp:, :] = jnp.where(j == pl.num_programs(1) - 1, 0.0, un_ref[...])
    pos = j * tp + lax.broadcasted_iota(jnp.int32, (tp, 1), 0)
    for g, w in enumerate(POOL_WINDOWS):
        cols = slice(g * POOL_GC, (g + 1) * POOL_GC)
        start = POOL_HALO - w // 2
        win = pe_scr[start:start + tp, cols]
        for k in range(1, w):
            win = win + pe_scr[start + k:start + k + tp, cols]
        lo = jnp.maximum(pos - w // 2, 0)
        hi = jnp.minimum(pos - w // 2 + w, seq)
        pooled = win / (hi - lo).astype(F32) - um_ref[:, cols]
        o_ref[:, cols] = (_bdot(pooled.astype(BF16), pw_ref[g]) * ps_ref[:, cols]).astype(BF16)


def _pool(u, pool_w, pool_scale, bsz):
    t = u.shape[0]
    seq = t // bsz
    tp = 256
    nt = seq // tp
    tiles8 = tp // SUBLANES
    return pl.pallas_call(
        functools.partial(_pool_kernel, seq=seq),
        grid=(bsz, nt),
        in_specs=[
            pl.BlockSpec((tp, POOL_W), lambda b, j: (b * nt + j, 0)),
            pl.BlockSpec((SUBLANES, POOL_W), lambda b, j: (jnp.maximum((b * nt + j) * tiles8 - 1, 0), 0)),
            pl.BlockSpec((SUBLANES, POOL_W),
                         lambda b, j: (jnp.minimum((b * nt + j + 1) * tiles8, t // SUBLANES - 1), 0)),
            pl.BlockSpec((POOL_GROUPS, POOL_GC, POOL_GC), lambda b, j: (0, 0, 0)),
            pl.BlockSpec((1, POOL_W), lambda b, j: (0, 0)),
        ],
        out_specs=pl.BlockSpec((tp, POOL_W), lambda b, j: (b * nt + j, 0)),
        out_shape=jax.ShapeDtypeStruct((t, POOL_W), BF16),
        scratch_shapes=[pltpu.VMEM((tp + 2 * POOL_HALO, POOL_W), F32)],
        compiler_params=_cparams(("parallel", "parallel"), 32),
        name="pool",
    )(u, u, u, pool_w, pool_scale)


def _rope(r, cos, sin_signed):
    lane = lax.broadcasted_iota(jnp.int32, r.shape, 1)
    quarter = MLA_ROPE // 4
    swapped = jnp.where(lane % (2 * quarter) < quarter,
                        pltpu.roll(r, LANES - quarter, 1), pltpu.roll(r, quarter, 1))
    return r * cos + swapped * sin_signed


def _rope_t(r, cos_t, sin_signed_t):
    feat = lax.broadcasted_iota(jnp.int32, r.shape, 0)
    quarter = MLA_ROPE // 4
    swapped = jnp.where(feat % (2 * quarter) < quarter,
                        pltpu.roll(r, LANES - quarter, 0), pltpu.roll(r, quarter, 0))
    return r * cos_t + swapped * sin_signed_t


def _mla_kernel(*refs, use_rope, with_q):
    if use_rope:
        cq_ref, ckv_ref, kr_ref, cos_ref, sin_ref, cos_t_ref, sin_t_ref = refs[:7]
        refs = refs[7:]
    else:
        cq_ref, ckv_ref, kr_ref = refs[:3]
        refs = refs[3:]
    qn_ref, wq_ref, kvn_ref, wk_t_ref, wv_ref, qhn_ref, khn_nope_ref, khn_rope_ref = refs[:8]
    outs = refs[8:]
    if with_q:
        q_ref, k_ref, v_ref = outs
    else:
        k_ref, v_ref = outs
    rope = (lambda r: _rope(r, cos_ref[...], sin_ref[...])) if use_rope else (lambda r: r)
    rope_t = (lambda r: _rope_t(r, cos_t_ref[...], sin_t_ref[...])) if use_rope else (lambda r: r)
    inv_qk = 1.0 / MLA_QK

    if with_q:
        qf = _bdot(_rms(cq_ref[...], qn_ref[...]).astype(BF16), wq_ref[...])
        scale = MLA_QK ** -0.5 * LOG2_E
        for h in range(MLA_HEADS):
            qh = qf[:, h * MLA_QKP:(h + 1) * MLA_QKP]
            rs = lax.rsqrt(jnp.sum(qh * qh, axis=-1, keepdims=True) * inv_qk + NORM_EPS) * scale
            qh = qh * rs * qhn_ref[...]
            q_ref[:, h * MLA_QKP:h * MLA_QKP + MLA_NOPE] = qh[:, :MLA_NOPE].astype(BF16)
            q_ref[:, h * MLA_QKP + MLA_NOPE:(h + 1) * MLA_QKP] = rope(qh[:, MLA_NOPE:]).astype(BF16)

    ckv_n = _rms(ckv_ref[...], kvn_ref[...])
    v_ref[...] = _bdot(ckv_n.astype(BF16), wv_ref[...]).astype(BF16)
    kn_t = _bdot(wk_t_ref[...], ckv_n.T.astype(BF16))
    kr_t = kr_ref[...].T
    ss_kr = jnp.sum(kr_t * kr_t, axis=0, keepdims=True)
    kr_w = rope_t(kr_t * khn_rope_ref[...])
    for h in range(MLA_HEADS):
        kn = kn_t[h * MLA_NOPE:(h + 1) * MLA_NOPE, :]
        rs = lax.rsqrt((jnp.sum(kn * kn, axis=0, keepdims=True) + ss_kr) * inv_qk + NORM_EPS)
        k_ref[h * MLA_QKP:h * MLA_QKP + MLA_NOPE, :] = (kn * rs * khn_nope_ref[...]).astype(BF16)
        k_ref[h * MLA_QKP + MLA_NOPE:(h + 1) * MLA_QKP, :] = (kr_w * rs).astype(BF16)


MLA_TM = 256


def _mla_proj(cq, ckv, kr, rope_tabs, qn, wq, kvn, wk_t, wv, qhn, khn, with_q):
    t = cq.shape[0]
    tm = MLA_TM
    use_rope = rope_tabs is not None
    row = lambda i: (i, 0)
    const = lambda i: (0, 0)
    ins = [cq, ckv, kr]
    in_specs = [pl.BlockSpec((tm, Q_LORA), row), pl.BlockSpec((tm, KV_LORA), row), pl.BlockSpec((tm, LANES), row)]
    if use_rope:
        cos, sin = rope_tabs
        nrt = cos.shape[0] // tm
        ins += [cos, sin, cos.T, sin.T]
        in_specs += [pl.BlockSpec((tm, LANES), lambda i: (i % nrt, 0))] * 2
        in_specs += [pl.BlockSpec((LANES, tm), lambda i: (0, i % nrt))] * 2
    khn_nope = jnp.broadcast_to(khn[:MLA_NOPE, None], (MLA_NOPE, tm))
    khn_rope = jnp.broadcast_to(khn[MLA_NOPE:, None], (MLA_QKP - MLA_NOPE, tm))
    ins += [qn, wq, kvn, wk_t, wv, qhn, khn_nope, khn_rope]
    in_specs += [pl.BlockSpec((1, Q_LORA), const), pl.BlockSpec((Q_LORA, MLA_HEADS * MLA_QKP), const),
                 pl.BlockSpec((1, KV_LORA), const), pl.BlockSpec((MLA_HEADS * MLA_NOPE, KV_LORA), const),
                 pl.BlockSpec((KV_LORA, MLA_W), const),
                 pl.BlockSpec((1, MLA_QKP), const), pl.BlockSpec((MLA_NOPE, tm), const),
                 pl.BlockSpec((MLA_QKP - MLA_NOPE, tm), const)]
    out_specs = [pl.BlockSpec((MLA_HEADS * MLA_QKP, tm), lambda i: (0, i)), pl.BlockSpec((tm, MLA_W), row)]
    out_shape = [jax.ShapeDtypeStruct((MLA_HEADS * MLA_QKP, t), BF16), jax.ShapeDtypeStruct((t, MLA_W), BF16)]
    if with_q:
        out_specs = [pl.BlockSpec((tm, MLA_HEADS * MLA_QKP), row)] + out_specs
        out_shape = [jax.ShapeDtypeStruct((t, MLA_HEADS * MLA_QKP), BF16)] + out_shape
    return pl.pallas_call(
        functools.partial(_mla_kernel, use_rope=use_rope, with_q=with_q),
        grid=(t // tm,),
        in_specs=in_specs,
        out_specs=out_specs,
        out_shape=out_shape,
        compiler_params=_cparams(("parallel",), 40),
        name="mla_proj",
    )(*ins)


ATTN_CHAIN_ROWS = 256
ATTN_CHAINS = 2


def _attn_kernel(*refs, latent_keys, chains, sub):
    if latent_keys:
        q_ref, kc_ref, vc_ref, kl_ref, vl_ref, o_ref = refs
    else:
        q_ref, kc_ref, vc_ref, o_ref = refs

    def body(it, carry):
        for ch in range(chains):
            rows = pl.ds(pl.multiple_of((it * chains + ch) * sub, sub), sub)
            q = q_ref[rows, :]
            sc = _bdot(q, kc_ref[...])
            m = jnp.max(sc, axis=-1, keepdims=True)
            if latent_keys:
                sl = _bdot(q, kl_ref[...])
                m = jnp.maximum(m, jnp.max(sl, axis=-1, keepdims=True))
            pc = jnp.exp2(sc - m)
            den = jnp.sum(pc, axis=-1, keepdims=True)
            o = _bdot(pc.astype(BF16), vc_ref[...])
            if latent_keys:
                pl_ = jnp.exp2(sl - m)
                den = den + jnp.sum(pl_, axis=-1, keepdims=True)
                o = o + _bdot(pl_.astype(BF16), vl_ref[...])
            o_ref[rows, :] = (o / den).astype(BF16)
        return carry

    lax.fori_loop(0, q_ref.shape[0] // (chains * sub), body, 0)


def _attention(q, kc, vc, kl, vl, bsz):
    tq_total = q.shape[0]
    lq = tq_total // bsz
    lc = kc.shape[1] // bsz
    sub = min(ATTN_CHAIN_ROWS, lq)
    chains = min(ATTN_CHAINS, lq // sub)
    latent_keys = kl is not None
    ins = [q, kc, vc]
    in_specs = [pl.BlockSpec((lq, MLA_QKP), lambda b, h: (b, h)),
                pl.BlockSpec((MLA_QKP, lc), lambda b, h: (h, b)),
                pl.BlockSpec((lc, MLA_V), lambda b, h: (b, h))]
    if latent_keys:
        ll = kl.shape[1] // bsz
        ins += [kl, vl]
        in_specs += [pl.BlockSpec((MLA_QKP, ll), lambda b, h: (h, b)),
                     pl.BlockSpec((ll, MLA_V), lambda b, h: (b, h))]
    return pl.pallas_call(
        functools.partial(_attn_kernel, latent_keys=latent_keys, chains=chains, sub=sub),
        grid=(bsz, MLA_HEADS),
        in_specs=in_specs,
        out_specs=pl.BlockSpec((lq, MLA_V), lambda b, h: (b, h)),
        out_shape=jax.ShapeDtypeStruct((tq_total, MLA_W), BF16),
        compiler_params=_cparams(("parallel", "parallel"), 48),
        name="attention",
    )(*ins)


def _out_proj_kernel(*refs, route):
    ssd_ref, pool_ref, att_ref, h_ref, mod_ref, nw_ref, w_ref = refs[:7]
    refs = refs[7:]
    if route:
        wr_hi_ref, wr_lo_ref, ho_ref, v_ref, r_ref = refs
    else:
        ho_ref, v_ref = refs
    mod = mod_ref[0]
    for r0 in range(0, h_ref.shape[0], OUT_SUB):
        rows = slice(r0, r0 + OUT_SUB)
        mix = (_bdot(ssd_ref[rows, :], w_ref[0:SSD_W, :])
               + _bdot(pool_ref[rows, :], w_ref[SSD_W:SSD_W + POOL_W, :])
               + _bdot(att_ref[rows, :], w_ref[SSD_W + POOL_W:, :]))
        hn = h_ref[rows, :] + mod[2:3, :] * mix
        ho_ref[rows, :] = hn
        v = _rms(hn, nw_ref[...]) * (1.0 + mod[4:5, :]) + mod[3:4, :]
        vb = v.astype(BF16)
        v_ref[rows, :] = vb
        if route:
            v_lo = (v - vb.astype(F32)).astype(BF16)
            logits = _bdot(vb, wr_hi_ref[...]) + _bdot(v_lo, wr_hi_ref[...]) + _bdot(vb, wr_lo_ref[...])
            lane = lax.broadcasted_iota(jnp.int32, logits.shape, 1).astype(F32)
            neg = -jnp.inf
            logits = jnp.where(lane < N_EXPERTS, logits, neg)
            m1 = jnp.max(logits, axis=-1, keepdims=True)
            i1 = jnp.min(jnp.where(logits == m1, lane, float(LANES)), axis=-1, keepdims=True)
            rest = jnp.where(lane == i1, neg, logits)
            m2 = jnp.max(rest, axis=-1, keepdims=True)
            i2 = jnp.min(jnp.where(rest == m2, lane, float(LANES)), axis=-1, keepdims=True)
            e2 = jnp.exp(m2 - m1)
            g1 = 1.0 / (1.0 + e2)
            g2 = e2 / (1.0 + e2)
            r_ref[rows, :] = jnp.where(lane == 0, i1, jnp.where(lane == 1, i2, jnp.where(lane == 2, g1, jnp.where(lane == 3, g2, 0.0))))


OUT_TM = 512
OUT_SUB = 256


def _out_proj(ssd, pool, att, h, mod, norm_w, w, rows_per_mod, router=None):
    t, d = h.shape
    tm = OUT_TM
    tiles_per_mod = rows_per_mod // tm
    row = lambda i: (i, 0)
    const = lambda i: (0, 0)
    route = router is not None
    ins = [ssd, pool, att, h, mod, norm_w, w]
    in_specs = [pl.BlockSpec((tm, SSD_W), row), pl.BlockSpec((tm, POOL_W), row), pl.BlockSpec((tm, MLA_W), row),
                pl.BlockSpec((tm, d), row), pl.BlockSpec((1, 6, d), lambda i: (i // tiles_per_mod, 0, 0)),
                pl.BlockSpec((1, d), const), pl.BlockSpec((d, d), const)]
    out_specs = [pl.BlockSpec((tm, d), row), pl.BlockSpec((tm, d), row)]
    out_shape = [jax.ShapeDtypeStruct((t, d), F32), jax.ShapeDtypeStruct((t, d), BF16)]
    if route:
        ins += list(router)
        in_specs += [pl.BlockSpec((d, LANES), const)] * 2
        out_specs.append(pl.BlockSpec((tm, LANES), row))
        out_shape.append(jax.ShapeDtypeStruct((t, LANES), F32))
    return pl.pallas_call(
        functools.partial(_out_proj_kernel, route=route),
        grid=(t // tm,),
        in_specs=in_specs,
        out_specs=out_specs,
        out_shape=out_shape,
        compiler_params=_cparams(("parallel",), 48),
        name="out_proj",
    )(*ins)


FFN_TM = 1024
FFN_TF = 512
MOE_TF = 512
FFN_SUB = 512


FFN_CAST_CHUNK = 512


def _swiglu_tile(x_ref, wg_ref, wu_ref, wd_ref, o_ref, row_blocks):
    d = x_ref.shape[1]
    ck = FFN_CAST_CHUNK
    g = [None] * len(row_blocks)
    u = [None] * len(row_blocks)
    for k0 in range(0, d, ck):
        wg = wg_ref[k0:k0 + ck, :].astype(BF16)
        wu = wu_ref[k0:k0 + ck, :].astype(BF16)
        for s, rows in enumerate(row_blocks):
            x = x_ref[rows, k0:k0 + ck]
            tg = _bdot(x, wg)
            tu = _bdot(x, wu)
            g[s] = tg if g[s] is None else g[s] + tg
            u[s] = tu if u[s] is None else u[s] + tu
    acts = [(_silu(g[s]) * u[s]).astype(BF16) for s in range(len(row_blocks))]
    for n0 in range(0, d, ck):
        wd = wd_ref[:, n0:n0 + ck].astype(BF16)
        for s, rows in enumerate(row_blocks):
            o_ref[rows, n0:n0 + ck] += _bdot(acts[s], wd)


def _ffn_kernel(v_ref, wg_ref, wu_ref, wd_ref, h_ref, mod_ref, o_ref, *, sub):
    f = pl.program_id(1)

    @pl.when(f == 0)
    def _():
        o_ref[...] = jnp.zeros_like(o_ref)

    row_blocks = [slice(s * sub, (s + 1) * sub) for s in range(v_ref.shape[0] // sub)]
    _swiglu_tile(v_ref, wg_ref.at[0], wu_ref.at[0], wd_ref.at[0], o_ref, row_blocks)

    @pl.when(f == pl.num_programs(1) - 1)
    def _():
        o_ref[...] = h_ref[...] + mod_ref[0][5:6, :] * o_ref[...]


def _ffn(v, wg, wu, wd, j, h, mod, rows_per_mod):
    t, d = h.shape
    tm = min(FFN_TM, rows_per_mod)
    tf = FFN_TF
    tiles_per_mod = rows_per_mod // tm
    nf = wg.shape[2] // tf
    return pl.pallas_call(
        functools.partial(_ffn_kernel, sub=min(FFN_SUB, tm)),
        grid=(t // tm, nf),
        in_specs=[pl.BlockSpec((tm, d), lambda i, f: (i, 0), pipeline_mode=pl.Buffered(1)),
                  pl.BlockSpec((1, d, tf), lambda i, f: (j, 0, f)),
                  pl.BlockSpec((1, d, tf), lambda i, f: (j, 0, f)),
                  pl.BlockSpec((1, tf, d), lambda i, f: (j, f, 0)),
                  pl.BlockSpec((tm, d), lambda i, f: (i, 0), pipeline_mode=pl.Buffered(1)),
                  pl.BlockSpec((1, 6, d), lambda i, f: (i // tiles_per_mod, 0, 0))],
        out_specs=pl.BlockSpec((tm, d), lambda i, f: (i, 0)),
        out_shape=jax.ShapeDtypeStruct((t, d), F32),
        compiler_params=_cparams(("parallel", "arbitrary"), 58),
        name="ffn",
    )(v, wg, wu, wd, h, mod)


def _moe_kernel(te_ref, tn_ref, x_ref, wg_ref, wu_ref, wd_ref, o_ref, acc_scr, *, sub):
    i = pl.program_id(0)
    f = pl.program_id(1)
    n_rows = tn_ref[i]

    @pl.when(n_rows > 0)
    def _():
        @pl.when(f == 0)
        def _():
            acc_scr[...] = jnp.zeros_like(acc_scr)

        refs = (x_ref, wg_ref.at[0, 0], wu_ref.at[0, 0], wd_ref.at[0, 0], acc_scr)
        n_blocks = x_ref.shape[0] // sub
        for live in range(1, n_blocks + 1):
            lo = (live - 1) * sub
            cond = (n_rows > lo) if live == n_blocks else jnp.logical_and(n_rows > lo, n_rows <= lo + sub)

            @pl.when(cond)
            def _():
                _swiglu_tile(*refs, [slice(s * sub, (s + 1) * sub) for s in range(live)])

        @pl.when(f == pl.num_programs(1) - 1)
        def _():
            o_ref[...] = acc_scr[...].astype(BF16)


def _moe_experts(tile_expert, tile_rows, x_sorted, wg, wu, wd, j):
    r, d = x_sorted.shape
    tm, tf = FFN_TM, MOE_TF
    nf = wg.shape[3] // tf

    def fidx(i, f, tn):
        return jnp.where(tn[i] > 0, f, nf - 1)

    grid_spec = pltpu.PrefetchScalarGridSpec(
        num_scalar_prefetch=2,
        grid=(r // tm, nf),
        in_specs=[pl.BlockSpec((tm, d), lambda i, f, te, tn: (i, 0)),
                  pl.BlockSpec((1, 1, d, tf), lambda i, f, te, tn: (j, te[i], 0, fidx(i, f, tn))),
                  pl.BlockSpec((1, 1, d, tf), lambda i, f, te, tn: (j, te[i], 0, fidx(i, f, tn))),
                  pl.BlockSpec((1, 1, tf, d), lambda i, f, te, tn: (j, te[i], fidx(i, f, tn), 0))],
        out_specs=pl.BlockSpec((tm, d), lambda i, f, te, tn: (i, 0)),
        scratch_shapes=[pltpu.VMEM((tm, d), F32)],
    )
    return pl.pallas_call(
        functools.partial(_moe_kernel, sub=FFN_SUB),
        grid_spec=grid_spec,
        out_shape=jax.ShapeDtypeStruct((r, d), BF16),
        compiler_params=_cparams(("arbitrary", "arbitrary"), 58),
        name="moe_experts",
    )(tile_expert, tile_rows, x_sorted, wg, wu, wd)


def _combine_kernel(h_ref, y0_ref, y1_ref, r_ref, mod_ref, o_ref):
    r = r_ref[...]
    mixed = r[:, 2:3] * y0_ref[...].astype(F32) + r[:, 3:4] * y1_ref[...].astype(F32)
    o_ref[...] = h_ref[...] + mod_ref[0][5:6, :] * mixed


def _combine(h, y0, y1, route, mod, rows_per_mod):
    t, d = h.shape
    tm = 512
    tiles_per_mod = rows_per_mod // tm
    row = lambda i: (i, 0)
    return pl.pallas_call(
        _combine_kernel,
        grid=(t // tm,),
        in_specs=[pl.BlockSpec((tm, d), row), pl.BlockSpec((tm, d), row), pl.BlockSpec((tm, d), row),
                  pl.BlockSpec((tm, LANES), row),
                  pl.BlockSpec((1, 6, d), lambda i: (i // tiles_per_mod, 0, 0))],
        out_specs=pl.BlockSpec((tm, d), row),
        out_shape=jax.ShapeDtypeStruct((t, d), F32),
        compiler_params=_cparams(("parallel",), 48),
        name="moe_combine",
    )(h, y0, y1, route, mod)


def _moe(v, route, h, mod, wg, wu, wd, j, rows_per_mod):
    t = v.shape[0]
    tm = FFN_TM
    n_assign = t * TOP_K
    rows = n_assign + N_EXPERTS * tm
    expert = route[:, :TOP_K].astype(jnp.int32).reshape(-1)
    onehot = (expert[:, None] == jnp.arange(N_EXPERTS, dtype=jnp.int32)[None, :]).astype(jnp.int32)
    csum = jnp.cumsum(onehot, axis=0)
    rank = jnp.sum(csum * onehot, axis=1) - 1
    counts = csum[-1]
    padded = ((counts + tm - 1) // tm) * tm
    ends = jnp.cumsum(padded)
    starts = ends - padded
    dest = starts[expert] + rank
    row_token = (jnp.arange(rows, dtype=jnp.int32) % t).at[dest].set(jnp.arange(n_assign, dtype=jnp.int32) // TOP_K)
    tile_start = jnp.arange(rows // tm, dtype=jnp.int32) * tm
    tile_expert = jnp.minimum(jnp.searchsorted(ends, tile_start, side="right"), N_EXPERTS - 1).astype(jnp.int32)
    tile_rows = jnp.clip((starts + counts)[tile_expert] - tile_start, 0, tm)
    tile_rows = jnp.where(tile_start < ends[-1], tile_rows, 0).astype(jnp.int32)
    last_valid = jnp.maximum(ends[-1] // tm - 1, 0)
    tile_expert = jnp.where(tile_rows > 0, tile_expert, tile_expert[last_valid])
    x_sorted = v.at[row_token].get(mode="promise_in_bounds")
    y_sorted = _moe_experts(tile_expert, tile_rows, x_sorted, wg, wu, wd, j)
    dest2 = dest.reshape(t, TOP_K)
    y0 = y_sorted.at[dest2[:, 0]].get(mode="promise_in_bounds")
    y1 = y_sorted.at[dest2[:, 1]].get(mode="promise_in_bounds")
    return _combine(h, y0, y1, route, mod, rows_per_mod)


def _pad_cols(a, n):
    return jnp.pad(a, ((0, 0), (0, n - a.shape[1])))


def _prep_w_in(w):
    z, xbc, dtr, pool, cq, ckv, kr = jnp.split(w, np.cumsum(IN_SIZES)[:-1].tolist(), axis=1)
    parts = [z, xbc, pool, cq, ckv, _pad_cols(kr, LANES), _pad_cols(dtr[:, :SSD_HEADS], LANES),
             _pad_cols(dtr[:, SSD_HEADS:], LANES)]
    return jnp.concatenate(parts, axis=1).astype(BF16)


def _prep_w_uq(w):
    w = w.reshape(Q_LORA, MLA_HEADS, MLA_QK)
    w = jnp.pad(w, ((0, 0), (0, 0), (0, MLA_QKP - MLA_QK)))
    return w.reshape(Q_LORA, MLA_HEADS * MLA_QKP).astype(BF16)


def _prep_w_ukv(w):
    w = w.reshape(KV_LORA, MLA_HEADS, MLA_NOPE + MLA_V)
    k = w[:, :, :MLA_NOPE].reshape(KV_LORA, MLA_HEADS * MLA_NOPE)
    v = w[:, :, MLA_NOPE:].reshape(KV_LORA, MLA_HEADS * MLA_V)
    return k.T.astype(BF16), v.astype(BF16)


def _rope_tables(seq):
    quarter = MLA_ROPE // 4
    inv = ROPE_BASE ** (-jnp.arange(quarter, dtype=F32) / quarter)
    pos = jnp.arange(seq, dtype=jnp.int32)
    pos_row = (pos // GRID_W).astype(F32)
    pos_col = (pos % GRID_W).astype(F32)
    ang_row = pos_row[:, None] * inv[None, :]
    ang_col = pos_col[:, None] * inv[None, :]
    ang = jnp.concatenate([ang_row, ang_row, ang_col, ang_col], axis=1)
    sign = jnp.tile(jnp.concatenate([-jnp.ones((quarter,), F32), jnp.ones((quarter,), F32)]), 2)
    cos = jnp.concatenate([jnp.cos(ang), jnp.ones((seq, LANES - MLA_ROPE), F32)], axis=1)
    sin = jnp.concatenate([jnp.sin(ang) * sign[None, :], jnp.zeros((seq, LANES - MLA_ROPE), F32)], axis=1)
    return cos, sin


def _head_expand():
    k = np.arange(LANES)[:, None]
    j = np.arange(SSD_W)[None, :]
    return jnp.asarray((k == j // SSD_HEADDIM).astype(np.float32), dtype=BF16)


def _split_hi_lo(w):
    hi = w.astype(BF16)
    lo = (w - hi.astype(F32)).astype(BF16)
    return hi, lo


def kernel(x, c, ctx, c_ctx, ada_w, ada_b, norm_mix_w, norm_ffn_w, w_in, ssd_conv_w, ssd_conv_b, ssd_dt_bias, ssd_a_log, ssd_d, ssd_norm_w, pool_w, pool_scale, mla_q_norm_w, mla_w_uq, mla_kv_norm_w, mla_w_ukv, mla_q_head_norm_w, mla_k_head_norm_w, w_out, ffn_w_gate, ffn_w_up, ffn_w_down, moe_w_router, moe_w_gate, moe_w_up, moe_w_down):
    bsz, seq, d = x.shape
    lc = ctx.shape[1]
    depth = ada_w.shape[0]
    h_l = x.reshape(bsz * seq, d)
    h_c = ctx.reshape(bsz * lc, d)
    t_l, t_c = h_l.shape[0], h_c.shape[0]

    pad_rows = (-(bsz + 1)) % SUBLANES
    cvec = jnp.concatenate([c, c_ctx[None, :], jnp.zeros((pad_rows, d), F32)], axis=0)
    rope_tabs = _rope_tables(seq)
    expand = _head_expand()
    zero_state = jnp.zeros((bsz, 2, SSD_GROUPS, SSD_STATE, SSD_GW), F32)

    for i in range(depth):
        last = i == depth - 1
        j = i // 2
        mod = _ada(cvec, ada_w, ada_b[i][None, :], i)
        mod_l = mod[:bsz].reshape(bsz, 6, d)
        mod_c = mod[bsz:bsz + 1].reshape(1, 6, d)

        w_in_i = _prep_w_in(w_in[i])
        nmw = norm_mix_w[i][None, :]
        conv_args = (ssd_conv_w[i], ssd_conv_b[i][None, :])
        z_l, xbc_l, pool_l, cq_l, ckv_l, kr_l, dt_l = _in_proj(h_l, mod_l, nmw, w_in_i, *conv_args, seq, seq)
        z_c, xbc_c, pool_c, cq_c, ckv_c, kr_c, dt_c = _in_proj(h_c, mod_c, nmw, w_in_i, *conv_args, t_c, lc)

        dtb = _pad_cols(ssd_dt_bias[i], LANES)[:, None, :]
        alog = _pad_cols(ssd_a_log[i], LANES)[:, None, :]
        dsk = jnp.repeat(ssd_d[i], SSD_HEADDIM)[None, :]
        snw = ssd_norm_w[i][None, :]
        ssd_args = (dtb, alog, dsk, snw, expand, bsz)
        ssd_c, st_c = _ssd(xbc_c, z_c, dt_c, zero_state, *ssd_args)
        ssd_l, _ = _ssd(xbc_l, z_l, dt_l, st_c, *ssd_args)

        pw = pool_w[i].astype(BF16)
        psc = pool_scale[i][None, :]
        pool_out_l = _pool(pool_l, pw, psc, bsz)

        qhn = _pad_cols(mla_q_head_norm_w[i][None, :], MLA_QKP)
        khn = _pad_cols(mla_k_head_norm_w[i][None, :], MLA_QKP)[0]
        mla_args = (mla_q_norm_w[i][None, :], _prep_w_uq(mla_w_uq[i]), mla_kv_norm_w[i][None, :],
                    *_prep_w_ukv(mla_w_ukv[i]), qhn, khn)
        q_l, k_l, v_l = _mla_proj(cq_l, ckv_l, kr_l, rope_tabs, *mla_args, with_q=True)
        if last:
            k_c, v_c = _mla_proj(cq_c, ckv_c, kr_c, None, *mla_args, with_q=False)
        else:
            q_c, k_c, v_c = _mla_proj(cq_c, ckv_c, kr_c, None, *mla_args, with_q=True)
        att_l = _attention(q_l, k_c, v_c, k_l, v_l, bsz)

        w_out_i = w_out[i].astype(BF16)
        nfw = norm_ffn_w[i][None, :]
        dense = i % 2 == 0
        router = None if dense else _split_hi_lo(_pad_cols(moe_w_router[j], LANES))
        res_l = _out_proj(ssd_l, pool_out_l, att_l, h_l, mod_l, nfw, w_out_i, seq, router)
        if dense:
            wg, wu, wd = ffn_w_gate, ffn_w_up, ffn_w_down
            h_l = _ffn(res_l[1], wg, wu, wd, j, res_l[0], mod_l, seq)
        else:
            wg, wu, wd = moe_w_gate, moe_w_up, moe_w_down
            h_l = _moe(res_l[1], res_l[2], res_l[0], mod_l, wg, wu, wd, j, seq)
        if not last:
            pool_out_c = _pool(pool_c, pw, psc, bsz)
            att_c = _attention(q_c, k_c, v_c, None, None, bsz)
            res_c = _out_proj(ssd_c, pool_out_c, att_c, h_c, mod_c, nfw, w_out_i, t_c, router)
            if dense:
                h_c = _ffn(res_c[1], wg, wu, wd, j, res_c[0], mod_c, t_c)
            else:
                h_c = _moe(res_c[1], res_c[2], res_c[0], mod_c, wg, wu, wd, j, t_c)
    return h_l.reshape(bsz, seq, d)
```
